```python
import math
import jax, jax.numpy as jnp
from jax import lax
import numpy as np

D_MODEL = 2048
BATCH = 4
SEQ = 2048
DEPTH = 4
DEC_BATCH = 128
DEC_SEQ = 4
PAST_LEN = 8192
PAGE_SIZE = 128

N_MIXERS = 3
N_MLA_LAYERS = (DEPTH + 2) // 3
N_POOL_LAYERS = (DEPTH + 1) // 3
N_S5_LAYERS = DEPTH // 3

D_FF = 5632
MACARON_WEIGHT = 0.5
NORM_EPS = 1e-6

N_HEADS = 16
Q_LORA = 512
KV_LORA = 512
QK_NOPE = 128
QK_ROPE = 64
V_DIM = 128
ROPE_THETA = 10000.0
Q_BLOCK = 128
ATTN_SCALE = 1.0 / math.sqrt(QK_NOPE + QK_ROPE)
NEG_INF = -1e30

POOL_WINDOWS = (2, 4, 8, 16)
POOL_GROUP = D_MODEL // len(POOL_WINDOWS)
POOL_HIST = max(POOL_WINDOWS) - 1

S5_GROUP = 16
S5_GROUPS = D_MODEL // S5_GROUP
S5_STATE = 64
DT_MIN = 0.001
DT_MAX = 0.1

kernel_name = 'hybrid_mla_pool_s5_macaron_step'


def rms_norm(x, g):
    xf = x.astype(jnp.float32)
    y = xf * lax.rsqrt(jnp.mean(xf * xf, axis=-1, keepdims=True) + NORM_EPS)
    return (y * g.astype(jnp.float32)).astype(x.dtype)


def half_ffn(h, g_pre, g_post, w_gate, w_up, w_down):
    u = rms_norm(h, g_pre)
    y = (jax.nn.silu(u @ w_gate) * (u @ w_up)) @ w_down
    return h + MACARON_WEIGHT * rms_norm(y, g_post)


def rope(x, pos):
    half = QK_ROPE // 2
    inv = ROPE_THETA ** (-jnp.arange(half, dtype=jnp.float32) / half)
    ang = pos.astype(jnp.float32)[:, None] * inv[None, :]
    ang = ang.reshape(ang.shape[:1] + (1,) * (x.ndim - 3) + (half,))
    cos, sin = jnp.cos(ang), jnp.sin(ang)
    xf = x.astype(jnp.float32)
    x1, x2 = xf[..., :half], xf[..., half:]
    return jnp.concatenate([x1 * cos - x2 * sin, x2 * cos + x1 * sin], axis=-1).astype(x.dtype)


def mla_project(u, pos, w_dq, g_q, w_uq, w_dkv, g_kv):
    q = rms_norm(u @ w_dq, g_q)
    q = jnp.einsum('btr,rhe->bthe', q, w_uq)
    q_nope = q[..., :QK_NOPE]
    q_rope = rope(q[..., QK_NOPE:], pos)
    kv = u @ w_dkv
    lat = rms_norm(kv[..., :KV_LORA], g_kv)
    k_rope = rope(kv[..., KV_LORA:], pos)
    return q_nope, q_rope, lat, k_rope


def mla_prompt_attn(q_nope, q_rope, lat, k_rope, w_uk, w_uv):
    b, t, h, _ = q_nope.shape
    k_nope = jnp.einsum('bkr,rhe->bkhe', lat, w_uk)
    v = jnp.einsum('bkr,rhv->bkhv', lat, w_uv)
    nblk = t // Q_BLOCK
    qn = q_nope.reshape(b, nblk, Q_BLOCK, h, QK_NOPE).swapaxes(0, 1)
    qr = q_rope.reshape(b, nblk, Q_BLOCK, h, QK_ROPE).swapaxes(0, 1)
    starts = jnp.arange(nblk) * Q_BLOCK
    kpos = jnp.arange(t)

    def block(args):
        qnb, qrb, start = args
        s = (jnp.einsum('bqhe,bkhe->bhqk', qnb, k_nope)
             + jnp.einsum('bqhe,bke->bhqk', qrb, k_rope)).astype(jnp.float32) * ATTN_SCALE
        qpos = start + jnp.arange(Q_BLOCK)
        s = jnp.where(kpos[None, :] <= qpos[:, None], s, NEG_INF)
        p = jax.nn.softmax(s, axis=-1).astype(v.dtype)
        return jnp.einsum('bhqk,bkhv->bqhv', p, v)

    o = lax.map(block, (qn, qr, starts))
    return o.swapaxes(0, 1).reshape(b, t, h * V_DIM)


def mla_sample_attn(q_nope, q_rope, lat_new, kr_new, cache_latent, cache_krope, layer, page_table, w_uk, w_uv):
    b, t = q_nope.shape[:2]
    n_past = page_table.shape[1] * PAGE_SIZE
    lat_past = cache_latent[layer, page_table].reshape(b, n_past, KV_LORA).astype(lat_new.dtype)
    kr_past = cache_krope[layer, page_table].reshape(b, n_past, QK_ROPE).astype(kr_new.dtype)
    q_lat = jnp.einsum('bqhe,rhe->bqhr', q_nope, w_uk)

    def scores(lat, kr):
        s = jnp.einsum('bqhr,bkr->bhqk', q_lat, lat) + jnp.einsum('bqhe,bke->bhqk', q_rope, kr)
        return s.astype(jnp.float32) * ATTN_SCALE

    s_past = scores(lat_past, kr_past)
    causal = jnp.tril(jnp.ones((t, t), dtype=bool))
    s_new = jnp.where(causal, scores(lat_new, kr_new), NEG_INF)
    p = jax.nn.softmax(jnp.concatenate([s_past, s_new], axis=-1), axis=-1).astype(lat_new.dtype)
    o_lat = (jnp.einsum('bhqk,bkr->bqhr', p[..., :n_past], lat_past)
             + jnp.einsum('bhqk,bkr->bqhr', p[..., n_past:], lat_new))
    o = jnp.einsum('bqhr,rhv->bqhv', o_lat, w_uv)
    return o.reshape(b, t, N_HEADS * V_DIM)


def pool_mix(u, hist, start_pos, w_grp, scale):
    b, t, d = u.shape
    ext = jnp.concatenate([hist.astype(u.dtype), u], axis=1)
    cs = lax.cumsum(ext.astype(jnp.float32), axis=1)
    cs = jnp.pad(cs, ((0, 0), (1, 0), (0, 0)))
    pos = start_pos + jnp.arange(t)
    parts = []
    for g, w in enumerate(POOL_WINDOWS):
        c0, c1 = g * POOL_GROUP, (g + 1) * POOL_GROUP
        win = (cs[:, POOL_HIST + 1:POOL_HIST + 1 + t, c0:c1]
               - cs[:, POOL_HIST + 1 - w:POOL_HIST + 1 - w + t, c0:c1])
        cnt = jnp.minimum(pos + 1, w).astype(jnp.float32)
        parts.append(win / cnt[None, :, None])
    pooled = jnp.concatenate(parts, axis=-1) - u.astype(jnp.float32)
    mixed = jnp.einsum('btgc,gce->btge', pooled.reshape(b, t, len(POOL_WINDOWS), POOL_GROUP),
                       w_grp.astype(jnp.float32)).reshape(b, t, d)
    return (mixed * scale.astype(jnp.float32)).astype(u.dtype), ext[:, -POOL_HIST:]


def cmul(ar, ai, br, bi):
    return ar * br - ai * bi, ar * bi + ai * br


def s5_mix(u, h0_re, h0_im, lam_re, lam_im, log_dt, b_re, b_im, c_re, c_im, d_skip, w_glu):
    f32 = jnp.float32
    b, t, d = u.shape
    lr, li = lam_re.astype(f32), lam_im.astype(f32)
    dt = jnp.exp(log_dt.astype(f32))[:, None]
    mag = jnp.exp(lr * dt)
    abar_re, abar_im = mag * jnp.cos(li * dt), mag * jnp.sin(li * dt)
    den = lr * lr + li * li
    nr, ni = abar_re - 1.0, abar_im
    zoh_re, zoh_im = (nr * lr + ni * li) / den, (ni * lr - nr * li) / den
    bb_re, bb_im = cmul(zoh_re[..., None], zoh_im[..., None], b_re.astype(f32), b_im.astype(f32))
    ug = u.astype(f32).reshape(b, t, S5_GROUPS, S5_GROUP)
    bu_re = jnp.einsum('btgc,gpc->btgp', ug, bb_re)
    bu_im = jnp.einsum('btgc,gpc->btgp', ug, bb_im)
    i_re, i_im = cmul(abar_re, abar_im, h0_re.astype(f32), h0_im.astype(f32))
    bu_re = bu_re.at[:, 0].add(i_re)
    bu_im = bu_im.at[:, 0].add(i_im)
    a_re = jnp.broadcast_to(abar_re, (1, t) + abar_re.shape)
    a_im = jnp.broadcast_to(abar_im, (1, t) + abar_im.shape)

    def combine(e1, e2):
        a1r, a1i, b1r, b1i = e1
        a2r, a2i, b2r, b2i = e2
        ar, ai = cmul(a2r, a2i, a1r, a1i)
        br, bi = cmul(a2r, a2i, b1r, b1i)
        return ar, ai, br + b2r, bi + b2i

    _, _, h_re, h_im = lax.associative_scan(combine, (a_re, a_im, bu_re, bu_im), axis=1)
    y = (jnp.einsum('btgp,gcp->btgc', h_re, c_re.astype(f32))
         - jnp.einsum('btgp,gcp->btgc', h_im, c_im.astype(f32))
         + d_skip.astype(f32).reshape(S5_GROUPS, S5_GROUP) * ug)
    y = jax.nn.gelu(y.reshape(b, t, d))
    z = y @ w_glu.astype(f32)
    out = z[..., :d] * jax.nn.sigmoid(z[..., d:])
    return out.astype(u.dtype), h_re[:, -1], h_im[:, -1]


def setup_inputs(seed: int = 0) -> dict:
    key = jax.random.key(seed)
    keys = iter(jax.random.split(key, 48))

    def nrm(shape, scale=1.0):
        r = jax.random.normal(next(keys), shape, jnp.float32)
        return r if scale == 1.0 else r * scale

    n_pages = PAST_LEN // PAGE_SIZE
    n_used = DEC_BATCH * n_pages
    n_pool_pages = n_used + n_used // 4
    page_table = jax.random.permutation(next(keys), n_pool_pages)[:n_used].reshape(DEC_BATCH, n_pages).astype(jnp.int32)
    s5_log_dt = jax.random.uniform(next(keys), (N_S5_LAYERS, S5_GROUPS), jnp.float32,
                                   math.log(DT_MIN), math.log(DT_MAX))
    s5_lam_re = -0.5 + nrm((N_S5_LAYERS, S5_GROUPS, S5_STATE), 0.01)
    s5_lam_im = math.pi * jnp.arange(S5_STATE, dtype=jnp.float32) + nrm((N_S5_LAYERS, S5_GROUPS, S5_STATE), 0.01)
    return {
        'x_prompt': nrm((BATCH, SEQ, D_MODEL)),
        'x_sample': nrm((DEC_BATCH, DEC_SEQ, D_MODEL)),
        'cache_latent': nrm((N_MLA_LAYERS, n_pool_pages, PAGE_SIZE, KV_LORA)),
        'cache_krope': nrm((N_MLA_LAYERS, n_pool_pages, PAGE_SIZE, QK_ROPE)),
        'page_table': page_table,
        'state_pool': nrm((N_POOL_LAYERS, DEC_BATCH, POOL_HIST, D_MODEL)),
        'state_s5_re': nrm((N_S5_LAYERS, DEC_BATCH, S5_GROUPS, S5_STATE), 0.5),
        'state_s5_im': nrm((N_S5_LAYERS, DEC_BATCH, S5_GROUPS, S5_STATE), 0.5),
        'norm_gains': 1.0 + nrm((DEPTH, 6, D_MODEL), 0.05),
        'ffn_w_gate': nrm((DEPTH, 2, D_MODEL, D_FF), D_MODEL ** -0.5),
        'ffn_w_up': nrm((DEPTH, 2, D_MODEL, D_FF), D_MODEL ** -0.5),
        'ffn_w_down': nrm((DEPTH, 2, D_FF, D_MODEL), D_FF ** -0.5),
        'mla_w_dq': nrm((N_MLA_LAYERS, D_MODEL, Q_LORA), D_MODEL ** -0.5),
        'mla_g_q': 1.0 + nrm((N_MLA_LAYERS, Q_LORA), 0.05),
        'mla_w_uq': nrm((N_MLA_LAYERS, Q_LORA, N_HEADS, QK_NOPE + QK_ROPE), Q_LORA ** -0.5),
        'mla_w_dkv': nrm((N_MLA_LAYERS, D_MODEL, KV_LORA + QK_ROPE), D_MODEL ** -0.5),
        'mla_g_kv': 1.0 + nrm((N_MLA_LAYERS, KV_LORA), 0.05),
        'mla_w_uk': nrm((N_MLA_LAYERS, KV_LORA, N_HEADS, QK_NOPE), KV_LORA ** -0.5),
        'mla_w_uv': nrm((N_MLA_LAYERS, KV_LORA, N_HEADS, V_DIM), KV_LORA ** -0.5),
        'mla_w_o': nrm((N_MLA_LAYERS, N_HEADS * V_DIM, D_MODEL), (N_HEADS * V_DIM) ** -0.5),
        'pool_w': nrm((N_POOL_LAYERS, len(POOL_WINDOWS), POOL_GROUP, POOL_GROUP), POOL_GROUP ** -0.5),
        'pool_scale': 1.0 + nrm((N_POOL_LAYERS, D_MODEL), 0.1),
        's5_lam_re': s5_lam_re,
        's5_lam_im': s5_lam_im,
        's5_log_dt': s5_log_dt,
        's5_b_re': nrm((N_S5_LAYERS, S5_GROUPS, S5_STATE, S5_GROUP), (2 * S5_GROUP) ** -0.5),
        's5_b_im': nrm((N_S5_LAYERS, S5_GROUPS, S5_STATE, S5_GROUP), (2 * S5_GROUP) ** -0.5),
        's5_c_re': nrm((N_S5_LAYERS, S5_GROUPS, S5_GROUP, S5_STATE), S5_STATE ** -0.5),
        's5_c_im': nrm((N_S5_LAYERS, S5_GROUPS, S5_GROUP, S5_STATE), S5_STATE ** -0.5),
        's5_d': nrm((N_S5_LAYERS, D_MODEL)),
        's5_w_glu': nrm((N_S5_LAYERS, D_MODEL, 2 * D_MODEL), D_MODEL ** -0.5),
    }


def reference(x_prompt, x_sample, cache_latent, cache_krope, page_table, state_pool, state_s5_re, state_s5_im,
              norm_gains, ffn_w_gate, ffn_w_up, ffn_w_down,
              mla_w_dq, mla_g_q, mla_w_uq, mla_w_dkv, mla_g_kv, mla_w_uk, mla_w_uv, mla_w_o,
              pool_w, pool_scale,
              s5_lam_re, s5_lam_im, s5_log_dt, s5_b_re, s5_b_im, s5_c_re, s5_c_im, s5_d, s5_w_glu):
    pos_p = jnp.arange(x_prompt.shape[1])
    pos_s = PAST_LEN + jnp.arange(x_sample.shape[1])
    hp, hs = x_prompt, x_sample
    lat_p, kr_p, lat_s, kr_s = [], [], [], []
    pool_p, pool_s = [], []
    s5r_p, s5i_p, s5r_s, s5i_s = [], [], [], []
    for i in range(DEPTH):
        g = norm_gains[i]
        j = i // N_MIXERS
        kind = i % N_MIXERS
        hp = half_ffn(hp, g[0], g[1], ffn_w_gate[i, 0], ffn_w_up[i, 0], ffn_w_down[i, 0])
        hs = half_ffn(hs, g[0], g[1], ffn_w_gate[i, 0], ffn_w_up[i, 0], ffn_w_down[i, 0])
        up, us = rms_norm(hp, g[2]), rms_norm(hs, g[2])
        if kind == 0:
            proj = (mla_w_dq[j], mla_g_q[j], mla_w_uq[j], mla_w_dkv[j], mla_g_kv[j])
            qn, qr, lt, kr = mla_project(up, pos_p, *proj)
            mp = mla_prompt_attn(qn, qr, lt, kr, mla_w_uk[j], mla_w_uv[j]) @ mla_w_o[j]
            lat_p.append(lt)
            kr_p.append(kr)
            qn, qr, lt, kr = mla_project(us, pos_s, *proj)
            ms = mla_sample_attn(qn, qr, lt, kr, cache_latent, cache_krope, j, page_table,
                                 mla_w_uk[j], mla_w_uv[j]) @ mla_w_o[j]
            lat_s.append(lt)
            kr_s.append(kr)
        elif kind == 1:
            zero_hist = jnp.zeros((up.shape[0], POOL_HIST, D_MODEL), up.dtype)
            mp, st = pool_mix(up, zero_hist, 0, pool_w[j], pool_scale[j])
            pool_p.append(st)
            ms, st = pool_mix(us, state_pool[j], PAST_LEN, pool_w[j], pool_scale[j])
            pool_s.append(st)
        else:
            s5w = (s5_lam_re[j], s5_lam_im[j], s5_log_dt[j], s5_b_re[j], s5_b_im[j],
                   s5_c_re[j], s5_c_im[j], s5_d[j], s5_w_glu[j])
            h0 = jnp.zeros((up.shape[0], S5_GROUPS, S5_STATE), jnp.float32)
            mp, sr, si = s5_mix(up, h0, h0, *s5w)
            s5r_p.append(sr)
            s5i_p.append(si)
            ms, sr, si = s5_mix(us, state_s5_re[j], state_s5_im[j], *s5w)
            s5r_s.append(sr)
            s5i_s.append(si)
        hp = hp + rms_norm(mp, g[3])
        hs = hs + rms_norm(ms, g[3])
        hp = half_ffn(hp, g[4], g[5], ffn_w_gate[i, 1], ffn_w_up[i, 1], ffn_w_down[i, 1])
        hs = half_ffn(hs, g[4], g[5], ffn_w_gate[i, 1], ffn_w_up[i, 1], ffn_w_down[i, 1])
    return (hp, hs,
            jnp.stack(lat_p), jnp.stack(kr_p), jnp.stack(lat_s), jnp.stack(kr_s),
            jnp.stack(pool_p), jnp.stack(pool_s),
            jnp.stack(s5r_p), jnp.stack(s5i_p), jnp.stack(s5r_s), jnp.stack(s5i_s))
```

```python
import functools
import math

import jax
import jax.numpy as jnp
from jax import lax
from jax.experimental import pallas as pl
from jax.experimental.pallas import tpu as pltpu

D_MODEL = 2048
DEPTH = 4
PAST_LEN = 8192
PAGE_SIZE = 128
N_MIXERS = 3
D_FF = 5632
MACARON_WEIGHT = 0.5
NORM_EPS = 1e-6
N_HEADS = 16
Q_LORA = 512
KV_LORA = 512
QK_NOPE = 128
QK_ROPE = 64
V_DIM = 128
ROPE_THETA = 10000.0
Q_BLOCK = 128
ATTN_SCALE = 1.0 / math.sqrt(QK_NOPE + QK_ROPE)
NEG_INF = -1e30
POOL_WINDOWS = (2, 4, 8, 16)
POOL_GROUP = D_MODEL // len(POOL_WINDOWS)
POOL_HIST = max(POOL_WINDOWS) - 1
S5_GROUP = 16
S5_GROUPS = D_MODEL // S5_GROUP
S5_STATE = 64

F32 = jnp.float32
BF16 = jnp.bfloat16

V7X_VMEM_BYTES = 64 * 1024 * 1024
VMEM_LIMIT_BYTES = V7X_VMEM_BYTES - 8 * 1024 * 1024

DOWN_COLS = 512
NORM_ROWS = 128


def _rms_rows(x, gain):
    ms = jnp.mean(x * x, axis=-1, keepdims=True)
    return x * lax.rsqrt(ms + NORM_EPS) * gain


def _ffn_kernel(h_ref, gpre_ref, gpost_ref, wg_ref, wu_ref, wd_ref, o_ref, u_ref, *, n_f, tm):
    f = pl.program_id(1)
    n_chunks = tm // NORM_ROWS

    @pl.when(f == 0)
    def _():
        def body(c, carry):
            r = pl.multiple_of(c * NORM_ROWS, NORM_ROWS)
            x = h_ref[pl.ds(r, NORM_ROWS), :]
            u_ref[pl.ds(r, NORM_ROWS), :] = _rms_rows(x, gpre_ref[...]).astype(BF16)
            o_ref[pl.ds(r, NORM_ROWS), :] = jnp.zeros((NORM_ROWS, o_ref.shape[1]), F32)
            return carry
        lax.fori_loop(0, n_chunks, body, 0)

    u = u_ref[...]
    gate = jnp.dot(u, wg_ref[...].astype(BF16), preferred_element_type=F32)
    up = jnp.dot(u, wu_ref[...].astype(BF16), preferred_element_type=F32)
    act = (gate * jax.nn.sigmoid(gate) * up).astype(BF16)
    wd = wd_ref[...].astype(BF16)
    d = o_ref.shape[1]
    for c in range(d // DOWN_COLS):
        cols = slice(c * DOWN_COLS, (c + 1) * DOWN_COLS)
        o_ref[:, cols] += jnp.dot(act, wd[:, cols], preferred_element_type=F32)

    @pl.when(f == n_f - 1)
    def _():
        def body(c, carry):
            r = pl.multiple_of(c * NORM_ROWS, NORM_ROWS)
            y = o_ref[pl.ds(r, NORM_ROWS), :]
            o_ref[pl.ds(r, NORM_ROWS), :] = (
                h_ref[pl.ds(r, NORM_ROWS), :] + MACARON_WEIGHT * _rms_rows(y, gpost_ref[...]))
            return carry
        lax.fori_loop(0, n_chunks, body, 0)


def _half_ffn(h, g_pre, g_post, w_gate, w_up, w_down, *, tm, tf):
    m, d = h.shape
    n_f = D_FF // tf
    kern = functools.partial(_ffn_kernel, n_f=n_f, tm=tm)
    return pl.pallas_call(
        kern,
        grid=(m // tm, n_f),
        in_specs=[
            pl.BlockSpec((tm, d), lambda i, f: (i, 0)),
            pl.BlockSpec((1, d), lambda i, f: (0, 0)),
            pl.BlockSpec((1, d), lambda i, f: (0, 0)),
            pl.BlockSpec((d, tf), lambda i, f: (0, f)),
            pl.BlockSpec((d, tf), lambda i, f: (0, f)),
            pl.BlockSpec((tf, d), lambda i, f: (f, 0)),
        ],
        out_specs=pl.BlockSpec((tm, d), lambda i, f: (i, 0)),
        out_shape=jax.ShapeDtypeStruct((m, d), F32),
        scratch_shapes=[pltpu.VMEM((tm, d), BF16)],
        compiler_params=pltpu.CompilerParams(
            dimension_semantics=("arbitrary", "arbitrary"),
            vmem_limit_bytes=VMEM_LIMIT_BYTES),
        name="half_ffn",
    )(h, g_pre.reshape(1, d), g_post.reshape(1, d), w_gate, w_up, w_down)


def rms_norm(x, g):
    xf = x.astype(jnp.float32)
    y = xf * lax.rsqrt(jnp.mean(xf * xf, axis=-1, keepdims=True) + NORM_EPS)
    return (y * g.astype(jnp.float32)).astype(x.dtype)


def rope(x, pos):
    half = QK_ROPE // 2
    inv = ROPE_THETA ** (-jnp.arange(half, dtype=jnp.float32) / half)
    ang = pos.astype(jnp.float32)[:, None] * inv[None, :]
    ang = ang.reshape(ang.shape[:1] + (1,) * (x.ndim - 3) + (half,))
    cos, sin = jnp.cos(ang), jnp.sin(ang)
    xf = x.astype(jnp.float32)
    x1, x2 = xf[..., :half], xf[..., half:]
    return jnp.concatenate([x1 * cos - x2 * sin, x2 * cos + x1 * sin], axis=-1).astype(x.dtype)


def mla_project(u, pos, w_dq, g_q, w_uq, w_dkv, g_kv):
    q = rms_norm(u @ w_dq, g_q)
    q = jnp.einsum('btr,rhe->bthe', q, w_uq)
    q_nope = q[..., :QK_NOPE]
    q_rope = rope(q[..., QK_NOPE:], pos)
    kv = u @ w_dkv
    lat = rms_norm(kv[..., :KV_LORA], g_kv)
    k_rope = rope(kv[..., KV_LORA:], pos)
    return q_nope, q_rope, lat, k_rope


def mla_prompt_attn(q_nope, q_rope, lat, k_rope, w_uk, w_uv):
    b, t, h, _ = q_nope.shape
    k_nope = jnp.einsum('bkr,rhe->bkhe', lat, w_uk)
    v = jnp.einsum('bkr,rhv->bkhv', lat, w_uv)
    nblk = t // Q_BLOCK
    qn = q_nope.reshape(b, nblk, Q_BLOCK, h, QK_NOPE).swapaxes(0, 1)
    qr = q_rope.reshape(b, nblk, Q_BLOCK, h, QK_ROPE).swapaxes(0, 1)
    starts = jnp.arange(nblk) * Q_BLOCK
    kpos = jnp.arange(t)

    def block(args):
        qnb, qrb, start = args
        s = (jnp.einsum('bqhe,bkhe->bhqk', qnb, k_nope)
             + jnp.einsum('bqhe,bke->bhqk', qrb, k_rope)).astype(jnp.float32) * ATTN_SCALE
        qpos = start + jnp.arange(Q_BLOCK)
        s = jnp.where(kpos[None, :] <= qpos[:, None], s, NEG_INF)
        p = jax.nn.softmax(s, axis=-1).astype(v.dtype)
        return jnp.einsum('bhqk,bkhv->bqhv', p, v)

    o = lax.map(block, (qn, qr, starts))
    return o.swapaxes(0, 1).reshape(b, t, h * V_DIM)


def mla_sample_attn(q_nope, q_rope, lat_new, kr_new, cache_latent, cache_krope, layer, page_table, w_uk, w_uv):
    b, t = q_nope.shape[:2]
    n_past = page_table.shape[1] * PAGE_SIZE
    lat_past = cache_latent[layer, page_table].reshape(b, n_past, KV_LORA).astype(lat_new.dtype)
    kr_past = cache_krope[layer, page_table].reshape(b, n_past, QK_ROPE).astype(kr_new.dtype)
    q_lat = jnp.einsum('bqhe,rhe->bqhr', q_nope, w_uk)

    def scores(lat, kr):
        s = jnp.einsum('bqhr,bkr->bhqk', q_lat, lat) + jnp.einsum('bqhe,bke->bhqk', q_rope, kr)
        return s.astype(jnp.float32) * ATTN_SCALE

    s_past = scores(lat_past, kr_past)
    causal = jnp.tril(jnp.ones((t, t), dtype=bool))
    s_new = jnp.where(causal, scores(lat_new, kr_new), NEG_INF)
    p = jax.nn.softmax(jnp.concatenate([s_past, s_new], axis=-1), axis=-1).astype(lat_new.dtype)
    o_lat = (jnp.einsum('bhqk,bkr->bqhr', p[..., :n_past], lat_past)
             + jnp.einsum('bhqk,bkr->bqhr', p[..., n_past:], lat_new))
    o = jnp.einsum('bqhr,rhv->bqhv', o_lat, w_uv)
    return o.reshape(b, t, N_HEADS * V_DIM)


def pool_mix(u, hist, start_pos, w_grp, scale):
    b, t, d = u.shape
    ext = jnp.concatenate([hist.astype(u.dtype), u], axis=1)
    cs = lax.cumsum(ext.astype(jnp.float32), axis=1)
    cs = jnp.pad(cs, ((0, 0), (1, 0), (0, 0)))
    pos = start_pos + jnp.arange(t)
    parts = []
    for g, w in enumerate(POOL_WINDOWS):
        c0, c1 = g * POOL_GROUP, (g + 1) * POOL_GROUP
        win = (cs[:, POOL_HIST + 1:POOL_HIST + 1 + t, c0:c1]
               - cs[:, POOL_HIST + 1 - w:POOL_HIST + 1 - w + t, c0:c1])
        cnt = jnp.minimum(pos + 1, w).astype(jnp.float32)
        parts.append(win / cnt[None, :, None])
    pooled = jnp.concatenate(parts, axis=-1) - u.astype(jnp.float32)
    mixed = jnp.einsum('btgc,gce->btge', pooled.reshape(b, t, len(POOL_WINDOWS), POOL_GROUP),
                       w_grp.astype(jnp.float32)).reshape(b, t, d)
    return (mixed * scale.astype(jnp.float32)).astype(u.dtype), ext[:, -POOL_HIST:]


def cmul(ar, ai, br, bi):
    return ar * br - ai * bi, ar * bi + ai * br


def s5_mix(u, h0_re, h0_im, lam_re, lam_im, log_dt, b_re, b_im, c_re, c_im, d_skip, w_glu):
    f32 = jnp.float32
    b, t, d = u.shape
    lr, li = lam_re.astype(f32), lam_im.astype(f32)
    dt = jnp.exp(log_dt.astype(f32))[:, None]
    mag = jnp.exp(lr * dt)
    abar_re, abar_im = mag * jnp.cos(li * dt), mag * jnp.sin(li * dt)
    den = lr * lr + li * li
    nr, ni = abar_re - 1.0, abar_im
    zoh_re, zoh_im = (nr * lr + ni * li) / den, (ni * lr - nr * li) / den
    bb_re, bb_im = cmul(zoh_re[..., None], zoh_im[..., None], b_re.astype(f32), b_im.astype(f32))
    ug = u.astype(f32).reshape(b, t, S5_GROUPS, S5_GROUP)
    bu_re = jnp.einsum('btgc,gpc->btgp', ug, bb_re)
    bu_im = jnp.einsum('btgc,gpc->btgp', ug, bb_im)
    i_re, i_im = cmul(abar_re, abar_im, h0_re.astype(f32), h0_im.astype(f32))
    bu_re = bu_re.at[:, 0].add(i_re)
    bu_im = bu_im.at[:, 0].add(i_im)
    a_re = jnp.broadcast_to(abar_re, (1, t) + abar_re.shape)
    a_im = jnp.broadcast_to(abar_im, (1, t) + abar_im.shape)

    def combine(e1, e2):
        a1r, a1i, b1r, b1i = e1
        a2r, a2i, b2r, b2i = e2
        ar, ai = cmul(a2r, a2i, a1r, a1i)
        br, bi = cmul(a2r, a2i, b1r, b1i)
        return ar, ai, br + b2r, bi + b2i

    _, _, h_re, h_im = lax.associative_scan(combine, (a_re, a_im, bu_re, bu_im), axis=1)
    y = (jnp.einsum('btgp,gcp->btgc', h_re, c_re.astype(f32))
         - jnp.einsum('btgp,gcp->btgc', h_im, c_im.astype(f32))
         + d_skip.astype(f32).reshape(S5_GROUPS, S5_GROUP) * ug)
    y = jax.nn.gelu(y.reshape(b, t, d))
    z = y @ w_glu.astype(f32)
    out = z[..., :d] * jax.nn.sigmoid(z[..., d:])
    return out.astype(u.dtype), h_re[:, -1], h_im[:, -1]


def kernel(x_prompt, x_sample, cache_latent, cache_krope, page_table, state_pool, state_s5_re, state_s5_im,
           norm_gains, ffn_w_gate, ffn_w_up, ffn_w_down,
           mla_w_dq, mla_g_q, mla_w_uq, mla_w_dkv, mla_g_kv, mla_w_uk, mla_w_uv, mla_w_o,
           pool_w, pool_scale,
           s5_lam_re, s5_lam_im, s5_log_dt, s5_b_re, s5_b_im, s5_c_re, s5_c_im, s5_d, s5_w_glu):
    bp, tp, d = x_prompt.shape
    bs, ts, _ = x_sample.shape
    pos_p = jnp.arange(tp)
    pos_s = PAST_LEN + jnp.arange(ts)
    hp = x_prompt.reshape(bp * tp, d)
    hs = x_sample.reshape(bs * ts, d)
    lat_p, kr_p, lat_s, kr_s = [], [], [], []
    pool_p, pool_s = [], []
    s5r_p, s5i_p, s5r_s, s5i_s = [], [], [], []

    def ffn(h, g_pre, g_post, i, k, tm):
        return _half_ffn(h, g_pre, g_post, ffn_w_gate[i, k], ffn_w_up[i, k], ffn_w_down[i, k], tm=tm, tf=256)

    for i in range(DEPTH):
        g = norm_gains[i]
        j = i // N_MIXERS
        kind = i % N_MIXERS
        hp = ffn(hp, g[0], g[1], i, 0, 1024)
        hs = ffn(hs, g[0], g[1], i, 0, 512)
        up = rms_norm(hp, g[2]).reshape(bp, tp, d)
        us = rms_norm(hs, g[2]).reshape(bs, ts, d)
        if kind == 0:
            proj = (mla_w_dq[j], mla_g_q[j], mla_w_uq[j], mla_w_dkv[j], mla_g_kv[j])
            qn, qr, lt, kr = mla_project(up, pos_p, *proj)
            mp = mla_prompt_attn(qn, qr, lt, kr, mla_w_uk[j], mla_w_uv[j]) @ mla_w_o[j]
            lat_p.append(lt)
            kr_p.append(kr)
            qn, qr, lt, kr = mla_project(us, pos_s, *proj)
            ms = mla_sample_attn(qn, qr, lt, kr, cache_latent, cache_krope, j, page_table,
                                 mla_w_uk[j], mla_w_uv[j]) @ mla_w_o[j]
            lat_s.append(lt)
            kr_s.append(kr)
        elif kind == 1:
            zero_hist = jnp.zeros((bp, POOL_HIST, D_MODEL), up.dtype)
            mp, st = pool_mix(up, zero_hist, 0, pool_w[j], pool_scale[j])
            pool_p.append(st)
            ms, st = pool_mix(us, state_pool[j], PAST_LEN, pool_w[j], pool_scale[j])
            pool_s.append(st)
        else:
            s5w = (s5_lam_re[j], s5_lam_im[j], s5_log_dt[j], s5_b_re[j], s5_b_im[j],
                   s5_c_re[j], s5_c_im[j], s5_d[j], s5_w_glu[j])
            h0 = jnp.zeros((bp, S5_GROUPS, S5_STATE), jnp.float32)
            mp, sr, si = s5_mix(up, h0, h0, *s5w)
            s5r_p.append(sr)
            s5i_p.append(si)
            ms, sr, si = s5_mix(us, state_s5_re[j], state_s5_im[j], *s5w)
            s5r_s.append(sr)
            s5i_s.append(si)
        hp = hp + rms_norm(mp.reshape(bp * tp, d), g[3])
        hs = hs + rms_norm(ms.reshape(bs * ts, d), g[3])
        hp = ffn(hp, g[4], g[5], i, 1, 1024)
        hs = ffn(hs, g[4], g[5], i, 1, 512)
    return (hp.reshape(bp, tp, d), hs.reshape(bs, ts, d),
            jnp.stack(lat_p), jnp.stack(kr_p), jnp.stack(lat_s), jnp.stack(kr_s),
            jnp.stack(pool_p), jnp.stack(pool_s),
            jnp.stack(s5r_p), jnp.stack(s5i_p), jnp.stack(s5r_s), jnp.stack(s5i_s))
```

```python
import functools
import math

import jax
import jax.numpy as jnp
from jax import lax
from jax.experimental import pallas as pl
from jax.experimental.pallas import tpu as pltpu

D_MODEL = 2048
DEPTH = 4
PAST_LEN = 8192
PAGE_SIZE = 128
N_MIXERS = 3
D_FF = 5632
MACARON_WEIGHT = 0.5
NORM_EPS = 1e-6
N_HEADS = 16
Q_LORA = 512
KV_LORA = 512
QK_NOPE = 128
QK_ROPE = 64
V_DIM = 128
ROPE_THETA = 10000.0
ATTN_SCALE = 1.0 / math.sqrt(QK_NOPE + QK_ROPE)
NEG_INF = -1e30
POOL_WINDOWS = (2, 4, 8, 16)
POOL_GROUP = D_MODEL // len(POOL_WINDOWS)
POOL_HIST = max(POOL_WINDOWS) - 1
S5_GROUP = 16
S5_GROUPS = D_MODEL // S5_GROUP
S5_STATE = 64

F32 = jnp.float32
BF16 = jnp.bfloat16

V7X_VMEM_BYTES = 64 * 1024 * 1024
VMEM_LIMIT_BYTES = V7X_VMEM_BYTES - 8 * 1024 * 1024

DOWN_COLS = 512
NORM_ROWS = 128


def _rms_rows(x, gain):
    ms = jnp.mean(x * x, axis=-1, keepdims=True)
    return x * lax.rsqrt(ms + NORM_EPS) * gain


def _cparams(*sem):
    return pltpu.CompilerParams(dimension_semantics=sem, vmem_limit_bytes=VMEM_LIMIT_BYTES)


def _const_spec(shape):
    nd = len(shape)
    return pl.BlockSpec(shape, lambda *_: (0,) * nd, pipeline_mode=pl.Buffered(1))


def _dot(a, b):
    return jnp.dot(a, b, preferred_element_type=F32)


def _dot_nt(a, b):
    return lax.dot_general(a, b, (((1,), (1,)), ((), ())), preferred_element_type=F32)


def _ffn_kernel(h_ref, gpre_ref, gpost_ref, wg_ref, wu_ref, wd_ref, o_ref, u_ref, *, n_f, tm):
    f = pl.program_id(1)
    rows = min(NORM_ROWS, tm)
    n_chunks = tm // rows

    @pl.when(f == 0)
    def _():
        def body(c, carry):
            r = pl.multiple_of(c * rows, rows)
            x = h_ref[pl.ds(r, rows), :]
            u_ref[pl.ds(r, rows), :] = _rms_rows(x, gpre_ref[...]).astype(BF16)
            o_ref[pl.ds(r, rows), :] = jnp.zeros((rows, o_ref.shape[1]), F32)
            return carry
        lax.fori_loop(0, n_chunks, body, 0)

    u = u_ref[...]
    gate = _dot(u, wg_ref[...].astype(BF16))
    up = _dot(u, wu_ref[...].astype(BF16))
    act = (gate * jax.nn.sigmoid(gate) * up).astype(BF16)
    wd = wd_ref[...].astype(BF16)
    d = o_ref.shape[1]
    for c in range(d // DOWN_COLS):
        cols = slice(c * DOWN_COLS, (c + 1) * DOWN_COLS)
        o_ref[:, cols] += _dot(act, wd[:, cols])

    @pl.when(f == n_f - 1)
    def _():
        def body(c, carry):
            r = pl.multiple_of(c * rows, rows)
            y = o_ref[pl.ds(r, rows), :]
            o_ref[pl.ds(r, rows), :] = (
                h_ref[pl.ds(r, rows), :] + MACARON_WEIGHT * _rms_rows(y, gpost_ref[...]))
            return carry
        lax.fori_loop(0, n_chunks, body, 0)


def _half_ffn(h, g_pre, g_post, w_gate, w_up, w_down, *, tm, tf):
    m, d = h.shape
    n_f = D_FF // tf
    return pl.pallas_call(
        functools.partial(_ffn_kernel, n_f=n_f, tm=tm),
        grid=(m // tm, n_f),
        in_specs=[
            pl.BlockSpec((tm, d), lambda i, f: (i, 0)),
            pl.BlockSpec((1, d), lambda i, f: (0, 0)),
            pl.BlockSpec((1, d), lambda i, f: (0, 0)),
            pl.BlockSpec((d, tf), lambda i, f: (0, f)),
            pl.BlockSpec((d, tf), lambda i, f: (0, f)),
            pl.BlockSpec((tf, d), lambda i, f: (f, 0)),
        ],
        out_specs=pl.BlockSpec((tm, d), lambda i, f: (i, 0)),
        out_shape=jax.ShapeDtypeStruct((m, d), F32),
        scratch_shapes=[pltpu.VMEM((tm, d), BF16)],
        compiler_params=_cparams("arbitrary", "arbitrary"),
        name="half_ffn",
    )(h, g_pre.reshape(1, d), g_post.reshape(1, d), w_gate, w_up, w_down)


ROPE_LANES = 128


def _mla_proj_kernel(h_ref, g2_ref, gq_ref, gkv_ref, wcat_ref, wuq_ref, cos_ref, sin_ref,
                     qn_ref, qr_ref, lat_ref, latb_ref, kr_ref, kr2_ref):
    u = _rms_rows(h_ref[...], g2_ref[...]).astype(BF16)
    r = _dot(u, wcat_ref[...])
    c = _rms_rows(r[:, :Q_LORA], gq_ref[...]).astype(BF16)
    lat = _rms_rows(r[:, Q_LORA:Q_LORA + KV_LORA], gkv_ref[...])
    lat_ref[...] = lat
    latb_ref[...] = lat.astype(BF16)
    cos = cos_ref[...]
    sin = sin_ref[...]
    k0 = Q_LORA + KV_LORA
    kr2 = r[:, k0:k0 + ROPE_LANES] * cos + r[:, k0 + ROPE_LANES:k0 + 2 * ROPE_LANES] * sin
    kr_ref[...] = kr2[:, :QK_ROPE]
    kr2_ref[...] = kr2.astype(BF16)
    q = _dot(c, wuq_ref[...])
    n_nope = N_HEADS * QK_NOPE
    n_rope = N_HEADS * QK_ROPE
    qn_ref[...] = q[:, :n_nope].astype(BF16)
    reps = n_rope // ROPE_LANES
    cos_q = jnp.concatenate([cos] * reps, axis=1)
    sin_q = jnp.concatenate([sin] * reps, axis=1)
    qr = q[:, n_nope:n_nope + n_rope] * cos_q + q[:, n_nope + n_rope:] * sin_q
    qr_ref[...] = qr.astype(BF16)


def _mla_proj(h, g2, g_q, g_kv, wcat, wuq, cos_t, sin_t, *, tm):
    m, d = h.shape
    tab_blocks = cos_t.shape[0] // tm
    n_nope = N_HEADS * QK_NOPE
    n_rope = N_HEADS * QK_ROPE
    row = lambda n: pl.BlockSpec((tm, n), lambda i: (i, 0))
    tab = pl.BlockSpec((tm, ROPE_LANES), lambda i: (i % tab_blocks, 0))
    return pl.pallas_call(
        _mla_proj_kernel,
        grid=(m // tm,),
        in_specs=[row(d), _const_spec((1, d)), _const_spec((1, Q_LORA)), _const_spec((1, KV_LORA)),
                  _const_spec(wcat.shape), _const_spec(wuq.shape), tab, tab],
        out_specs=[row(n_nope), row(n_rope), row(KV_LORA), row(KV_LORA), row(QK_ROPE), row(ROPE_LANES)],
        out_shape=[jax.ShapeDtypeStruct((m, n_nope), BF16), jax.ShapeDtypeStruct((m, n_rope), BF16),
                   jax.ShapeDtypeStruct((m, KV_LORA), F32), jax.ShapeDtypeStruct((m, KV_LORA), BF16),
                   jax.ShapeDtypeStruct((m, QK_ROPE), F32), jax.ShapeDtypeStruct((m, ROPE_LANES), BF16)],
        compiler_params=_cparams("arbitrary"),
        name="mla_proj",
    )(h, g2.reshape(1, d), g_q.reshape(1, Q_LORA), g_kv.reshape(1, KV_LORA), wcat, wuq, cos_t, sin_t)


def _prompt_attn_kernel(qn_ref, qr_ref, latb_ref, kr2_ref, wuk_ref, wuv_ref, o_ref, k_sc, v_sc, *, tq):
    head = pl.program_id(1)
    qi = pl.program_id(2)
    t = latb_ref.shape[0]

    @pl.when(qi == 0)
    def _():
        lat = latb_ref[...]
        k_sc[:, :QK_NOPE] = _dot(lat, wuk_ref[0]).astype(BF16)
        lane = lax.broadcasted_iota(jnp.int32, (t, ROPE_LANES), 1)
        mine = (lane // QK_ROPE) == (head % 2)
        k_sc[:, QK_NOPE:] = jnp.where(mine, kr2_ref[...], jnp.zeros_like(kr2_ref[...]))
        v_sc[...] = _dot(lat, wuv_ref[0]).astype(BF16)

    q = jnp.concatenate([qn_ref[...], qr_ref[...]], axis=1)

    def block(j, carry, masked):
        m_prev, l_prev, acc = carry
        r0 = pl.multiple_of(j * tq, tq)
        s = _dot_nt(q, k_sc[pl.ds(r0, tq), :]) * ATTN_SCALE
        if masked:
            rows = lax.broadcasted_iota(jnp.int32, (tq, tq), 0)
            cols = lax.broadcasted_iota(jnp.int32, (tq, tq), 1)
            s = jnp.where(cols <= rows, s, NEG_INF)
        m_new = jnp.maximum(m_prev, jnp.max(s, axis=1, keepdims=True))
        alpha = jnp.exp(m_prev - m_new)
        p = jnp.exp(s - m_new)
        l_new = alpha * l_prev + jnp.sum(p, axis=1, keepdims=True)
        acc = alpha * acc + _dot(p.astype(BF16), v_sc[pl.ds(r0, tq), :])
        return m_new, l_new, acc

    init = (jnp.full((tq, 1), NEG_INF, F32), jnp.zeros((tq, 1), F32), jnp.zeros((tq, V_DIM), F32))
    carry = lax.fori_loop(0, qi, lambda j, c: block(j, c, False), init)
    _, l_fin, acc = block(qi, carry, True)
    o_ref[...] = (acc / l_fin).astype(BF16)


def _prompt_attn(qn, qr, latb, kr2, wuk_h, wuv_h, *, b, t, tq):
    nq = t // tq
    qrow = lambda bi, h, qi: (bi * nq + qi, h)
    return pl.pallas_call(
        functools.partial(_prompt_attn_kernel, tq=tq),
        grid=(b, N_HEADS, nq),
        in_specs=[
            pl.BlockSpec((tq, QK_NOPE), qrow),
            pl.BlockSpec((tq, ROPE_LANES), lambda bi, h, qi: (bi * nq + qi, h // 2)),
            pl.BlockSpec((t, KV_LORA), lambda bi, h, qi: (bi, 0)),
            pl.BlockSpec((t, ROPE_LANES), lambda bi, h, qi: (bi, 0)),
            pl.BlockSpec((1, KV_LORA, QK_NOPE), lambda bi, h, qi: (h, 0, 0)),
            pl.BlockSpec((1, KV_LORA, V_DIM), lambda bi, h, qi: (h, 0, 0)),
        ],
        out_specs=pl.BlockSpec((tq, V_DIM), qrow),
        out_shape=jax.ShapeDtypeStruct((b * t, N_HEADS * V_DIM), BF16),
        scratch_shapes=[pltpu.VMEM((t, QK_NOPE + ROPE_LANES), BF16), pltpu.VMEM((t, V_DIM), BF16)],
        compiler_params=_cparams("arbitrary", "arbitrary", "arbitrary"),
        name="prompt_attn",
    )(qn, qr, latb, kr2, wuk_h, wuv_h)


def _bmm_kernel(x_ref, w_ref, o_ref):
    o_ref[...] = _dot(x_ref[...], w_ref[...]).astype(o_ref.dtype)


def _absorb_q(qn, wuk_t):
    m = qn.shape[0]
    return pl.pallas_call(
        _bmm_kernel,
        grid=(N_HEADS,),
        in_specs=[pl.BlockSpec((m, QK_NOPE), lambda h: (0, h)),
                  pl.BlockSpec((None, QK_NOPE, KV_LORA), lambda h: (h, 0, 0))],
        out_specs=pl.BlockSpec((None, m, KV_LORA), lambda h: (h, 0, 0)),
        out_shape=jax.ShapeDtypeStruct((N_HEADS, m, KV_LORA), BF16),
        compiler_params=_cparams("arbitrary"),
        name="absorb_q",
    )(qn, wuk_t)


def _expand_o(o_lat, wuv_h):
    m = o_lat.shape[1]
    return pl.pallas_call(
        _bmm_kernel,
        grid=(N_HEADS,),
        in_specs=[pl.BlockSpec((None, m, KV_LORA), lambda h: (h, 0, 0)),
                  pl.BlockSpec((None, KV_LORA, V_DIM), lambda h: (h, 0, 0))],
        out_specs=pl.BlockSpec((m, V_DIM), lambda h: (0, h)),
        out_shape=jax.ShapeDtypeStruct((m, N_HEADS * V_DIM), BF16),
        compiler_params=_cparams("arbitrary"),
        name="expand_o",
    )(o_lat, wuv_h)


PAGES_PER_STEP = 16


def _sample_attn_kernel(pt_ref, ql_ref, qr_ref, latn_ref, krn_ref, *rest, n_steps, t_new):
    del pt_ref
    lat_refs = rest[:PAGES_PER_STEP]
    kr_refs = rest[PAGES_PER_STEP:2 * PAGES_PER_STEP]
    o_ref, m_sc, l_sc, acc_sc = rest[2 * PAGES_PER_STEP:]
    j = pl.program_id(1)
    rows = ql_ref.shape[1]

    @pl.when(j == 0)
    def _():
        m_sc[...] = jnp.full(m_sc.shape, NEG_INF, F32)
        l_sc[...] = jnp.zeros(l_sc.shape, F32)
        acc_sc[...] = jnp.zeros(acc_sc.shape, F32)

    ql = ql_ref[0]
    qr = qr_ref[0]
    lats = [lat_refs[i][0, 0].astype(BF16) for i in range(PAGES_PER_STEP)]
    s = jnp.concatenate(
        [_dot_nt(ql, lats[i]) + _dot_nt(qr, kr_refs[i][0, 0].astype(BF16)) for i in range(PAGES_PER_STEP)],
        axis=1) * ATTN_SCALE
    m_prev = m_sc[...]
    m_new = jnp.maximum(m_prev, jnp.max(s, axis=1, keepdims=True))
    alpha = jnp.exp(m_prev - m_new)
    p = jnp.exp(s - m_new)
    l_new = alpha * l_sc[...] + jnp.sum(p, axis=1, keepdims=True)
    pb = p.astype(BF16)
    pv = _dot(pb[:, :PAGE_SIZE], lats[0])
    for i in range(1, PAGES_PER_STEP):
        pv += _dot(pb[:, i * PAGE_SIZE:(i + 1) * PAGE_SIZE], lats[i])
    acc_new = alpha * acc_sc[...] + pv
    m_sc[...] = m_new
    l_sc[...] = l_new
    acc_sc[...] = acc_new

    @pl.when(j == n_steps - 1)
    def _():
        latn = latn_ref[0].astype(BF16).astype(F32)
        krn = krn_ref[0].astype(BF16).astype(F32)
        qlf = ql.astype(F32)
        qrf = qr.astype(F32)
        tok = lax.broadcasted_iota(jnp.int32, (rows, 1), 0) % t_new
        s_new = []
        for k in range(t_new):
            sk = (jnp.sum(qlf * latn[k:k + 1, :], axis=1, keepdims=True)
                  + jnp.sum(qrf * krn[k:k + 1, :], axis=1, keepdims=True)) * ATTN_SCALE
            s_new.append(jnp.where(k <= tok, sk, NEG_INF))
        m_fin = m_new
        for sk in s_new:
            m_fin = jnp.maximum(m_fin, sk)
        a_fin = jnp.exp(m_new - m_fin)
        l_fin = a_fin * l_new
        acc = a_fin * acc_new
        for k in range(t_new):
            pk = jnp.exp(s_new[k] - m_fin)
            l_fin = l_fin + pk
            acc = acc + pk.astype(BF16).astype(F32) * latn[k:k + 1, :]
        o_ref[0] = (acc / l_fin).astype(BF16)


def _sample_attn(ql, qr, lat_new, kr_new, cache_latent, cache_krope, page_table, layer):
    b, rows, _ = ql.shape
    t_new = lat_new.shape[1]
    n_steps = page_table.shape[1] // PAGES_PER_STEP

    def page(i):
        return lambda bi, j, pt: (layer, pt[bi, j * PAGES_PER_STEP + i], 0, 0)

    per_seq = lambda n, w: pl.BlockSpec((1, n, w), lambda bi, j, pt: (bi, 0, 0))
    grid_spec = pltpu.PrefetchScalarGridSpec(
        num_scalar_prefetch=1,
        grid=(b, n_steps),
        in_specs=[per_seq(rows, KV_LORA), per_seq(rows, QK_ROPE), per_seq(t_new, KV_LORA), per_seq(t_new, QK_ROPE)]
        + [pl.BlockSpec((1, 1, PAGE_SIZE, KV_LORA), page(i)) for i in range(PAGES_PER_STEP)]
        + [pl.BlockSpec((1, 1, PAGE_SIZE, QK_ROPE), page(i)) for i in range(PAGES_PER_STEP)],
        out_specs=per_seq(rows, KV_LORA),
        scratch_shapes=[pltpu.VMEM((rows, 1), F32), pltpu.VMEM((rows, 1), F32), pltpu.VMEM((rows, KV_LORA), F32)],
    )
    return pl.pallas_call(
        functools.partial(_sample_attn_kernel, n_steps=n_steps, t_new=t_new),
        grid_spec=grid_spec,
        out_shape=jax.ShapeDtypeStruct((b, rows, KV_LORA), BF16),
        compiler_params=_cparams("arbitrary", "arbitrary"),
        name="sample_attn",
    )(page_table, ql, qr, lat_new, kr_new, *([cache_latent] * PAGES_PER_STEP), *([cache_krope] * PAGES_PER_STEP))


def _out_proj_kernel(x_ref, w_ref, h_ref, g_ref, o_ref):
    y = _dot(x_ref[...], w_ref[...])
    o_ref[...] = h_ref[...] + _rms_rows(y, g_ref[...])


def _out_proj(x, w, h, g, *, tm):
    m, d = h.shape
    row = lambda n: pl.BlockSpec((tm, n), lambda i: (i, 0))
    return pl.pallas_call(
        _out_proj_kernel,
        grid=(m // tm,),
        in_specs=[row(x.shape[1]), _const_spec(w.shape), row(d), _const_spec((1, d))],
        out_specs=row(d),
        out_shape=jax.ShapeDtypeStruct((m, d), F32),
        compiler_params=_cparams("arbitrary"),
        name="out_proj",
    )(x, w, h, g.reshape(1, d))


HIST_ROWS = 16


def _pool_mix_cols(window_sum, u_cols, cnt, w_g, scale_cols):
    pooled = window_sum / cnt - u_cols
    return _dot(pooled.astype(BF16), w_g) * scale_cols


def _pool_prompt_kernel(h_ref, g2_ref, g3_ref, scale_ref, w_ref, o_ref, st_ref, ubuf, *, tm, n_t):
    ti = pl.program_id(1)

    @pl.when(ti == 0)
    def _():
        ubuf[0:HIST_ROWS, :] = jnp.zeros((HIST_ROWS, ubuf.shape[1]), F32)

    @pl.when(ti > 0)
    def _():
        ubuf[0:HIST_ROWS, :] = ubuf[tm:tm + HIST_ROWS, :]

    h = h_ref[...]
    ubuf[HIST_ROWS:HIST_ROWS + tm, :] = _rms_rows(h, g2_ref[...])
    pos = ti * tm + lax.broadcasted_iota(jnp.int32, (tm, 1), 0)
    for g, w in enumerate(POOL_WINDOWS):
        cols = slice(g * POOL_GROUP, (g + 1) * POOL_GROUP)
        u_cols = ubuf[HIST_ROWS:HIST_ROWS + tm, cols]
        win = u_cols
        for k in range(1, w):
            win = win + ubuf[HIST_ROWS - k:HIST_ROWS - k + tm, cols]
        cnt = jnp.minimum(pos + 1, w).astype(F32)
        o_ref[:, cols] = _pool_mix_cols(win, u_cols, cnt, w_ref[g], scale_ref[:, cols])
    o_ref[...] = h + _rms_rows(o_ref[...], g3_ref[...])

    @pl.when(ti == n_t - 1)
    def _():
        st_ref[0] = ubuf[HIST_ROWS + tm - POOL_HIST:HIST_ROWS + tm, :]


def _pool_prompt(h, g2, g3, scale, w_bf, *, b, t, tm):
    d = h.shape[1]
    n_t = t // tm
    return pl.pallas_call(
        functools.partial(_pool_prompt_kernel, tm=tm, n_t=n_t),
        grid=(b, n_t),
        in_specs=[pl.BlockSpec((tm, d), lambda bi, ti: (bi * n_t + ti, 0)),
                  _const_spec((1, d)), _const_spec((1, d)), _const_spec((1, d)), _const_spec(w_bf.shape)],
        out_specs=[pl.BlockSpec((tm, d), lambda bi, ti: (bi * n_t + ti, 0)),
                   pl.BlockSpec((1, POOL_HIST, d), lambda bi, ti: (bi, 0, 0))],
        out_shape=[jax.ShapeDtypeStruct(h.shape, F32), jax.ShapeDtypeStruct((b, POOL_HIST, d), F32)],
        scratch_shapes=[pltpu.VMEM((HIST_ROWS + tm, d), F32)],
        compiler_params=_cparams("arbitrary", "arbitrary"),
        name="pool_prompt",
    )(h, g2.reshape(1, d), g3.reshape(1, d), scale.reshape(1, d), w_bf)


def _pool_sample_kernel(hist_ref, h_ref, g2_ref, g3_ref, scale_ref, w_ref, o_ref, st_ref, mix_sc, *, start_pos):
    t_new = h_ref.shape[0]
    nb = h_ref.shape[1]
    hs = [h_ref[t] for t in range(t_new)]
    ext = [hist_ref[k] for k in range(POOL_HIST)] + [_rms_rows(x, g2_ref[...]) for x in hs]
    for t in range(t_new):
        for g, w in enumerate(POOL_WINDOWS):
            cols = slice(g * POOL_GROUP, (g + 1) * POOL_GROUP)
            u_cols = ext[POOL_HIST + t][:, cols]
            win = u_cols
            for k in range(1, w):
                win = win + ext[POOL_HIST + t - k][:, cols]
            cnt = float(min(start_pos + t + 1, w))
            mix_sc[t * nb:(t + 1) * nb, cols] = _pool_mix_cols(win, u_cols, cnt, w_ref[g], scale_ref[:, cols])
    for t in range(t_new):
        o_ref[t] = hs[t] + _rms_rows(mix_sc[t * nb:(t + 1) * nb, :], g3_ref[...])
    for k in range(POOL_HIST):
        st_ref[k] = ext[t_new + k]


def _pool_sample(hist_tm, h_tm, g2, g3, scale, w_bf, *, nb, start_pos):
    t_new, b, d = h_tm.shape
    seq = lambda n: pl.BlockSpec((n, nb, d), lambda i: (0, i, 0))
    return pl.pallas_call(
        functools.partial(_pool_sample_kernel, start_pos=start_pos),
        grid=(b // nb,),
        in_specs=[seq(POOL_HIST), seq(t_new), _const_spec((1, d)), _const_spec((1, d)), _const_spec((1, d)),
                  _const_spec(w_bf.shape)],
        out_specs=[seq(t_new), seq(POOL_HIST)],
        out_shape=[jax.ShapeDtypeStruct(h_tm.shape, F32), jax.ShapeDtypeStruct(hist_tm.shape, F32)],
        scratch_shapes=[pltpu.VMEM((t_new * nb, d), F32)],
        compiler_params=_cparams("arbitrary"),
        name="pool_sample",
    )(hist_tm, h_tm, g2.reshape(1, d), g3.reshape(1, d), scale.reshape(1, d), w_bf)


def _s5_params_kernel(lr_ref, li_ref, logdt_ref, br_ref, bi_ref, ar_ref, ai_ref, bbr_ref, bbi_ref):
    lr = lr_ref[...]
    li = li_ref[...]
    dt = jnp.exp(logdt_ref[...])
    mag = jnp.exp(lr * dt)
    a_re = mag * jnp.cos(li * dt)
    a_im = mag * jnp.sin(li * dt)
    den = lr * lr + li * li
    nr = a_re - 1.0
    ni = a_im
    zoh_re = (nr * lr + ni * li) / den
    zoh_im = (ni * lr - nr * li) / den
    ar_ref[...] = a_re
    ai_ref[...] = a_im
    for c in range(S5_GROUP):
        b_re = br_ref[c]
        b_im = bi_ref[c]
        bbr_ref[c] = zoh_re * b_re - zoh_im * b_im
        bbi_ref[c] = zoh_re * b_im + zoh_im * b_re


def _s5_params(lam_re, lam_im, log_dt, b_re_cgp, b_im_cgp):
    g, p = lam_re.shape
    full = lambda shape: pl.BlockSpec(shape, lambda: (0,) * len(shape))
    gp = jax.ShapeDtypeStruct((g, p), F32)
    cgp = jax.ShapeDtypeStruct(b_re_cgp.shape, F32)
    return pl.pallas_call(
        _s5_params_kernel,
        in_specs=[full((g, p)), full((g, p)), full((g, 1)), full(b_re_cgp.shape), full(b_im_cgp.shape)],
        out_specs=[full((g, p)), full((g, p)), full(b_re_cgp.shape), full(b_re_cgp.shape)],
        out_shape=[gp, gp, cgp, cgp],
        name="s5_params",
    )(lam_re, lam_im, log_dt.reshape(g, 1), b_re_cgp, b_im_cgp)


def _rms_kernel(h_ref, g_ref, o_ref):
    o_ref[...] = _rms_rows(h_ref[...], g_ref[...])


def _rms_call(h, g, *, tm):
    m, d = h.shape
    row = pl.BlockSpec((tm, d), lambda i: (i, 0))
    return pl.pallas_call(
        _rms_kernel, grid=(m // tm,), in_specs=[row, _const_spec((1, d))], out_specs=row,
        out_shape=jax.ShapeDtypeStruct((m, d), F32), compiler_params=_cparams("arbitrary"), name="rms_rows",
    )(h, g.reshape(1, d))


S5_BLOCK_GROUPS = 16
S5_BLOCK_IN = S5_BLOCK_GROUPS * S5_GROUP
S5_BLOCK_STATE = S5_BLOCK_GROUPS * S5_STATE
S5_BLOCKS = S5_GROUPS // S5_BLOCK_GROUPS
SCAN_UNROLL = 8


def _s5_scan_kernel(u_ref, a_ref, b_ref, c_ref, d_ref, h0r_ref, h0i_ref, y_ref, hr_ref, hi_ref,
                    buf, cr_sc, ci_sc, *, rows_per_step, steps, n_chunks):
    c = pl.program_id(1)
    rps = rows_per_step
    ns = S5_BLOCK_STATE

    @pl.when(c == 0)
    def _():
        cr_sc[...] = h0r_ref[...]
        ci_sc[...] = h0i_ref[...]

    u = u_ref[...]
    buf[...] = _dot(u.astype(BF16), b_ref[0])
    a = a_ref[0]
    a_re = jnp.broadcast_to(a[:, :ns], (rps, ns))
    a_im = jnp.broadcast_to(a[:, ns:], (rps, ns))

    def step(t, carry):
        h_re, h_im = carry
        r0 = pl.multiple_of(t * rps, rps)
        n_re = a_re * h_re - a_im * h_im + buf[pl.ds(r0, rps), :ns]
        n_im = a_re * h_im + a_im * h_re + buf[pl.ds(r0, rps), ns:]
        buf[pl.ds(r0, rps), :ns] = n_re
        buf[pl.ds(r0, rps), ns:] = n_im
        return n_re, n_im

    carry = (cr_sc[...], ci_sc[...])
    if steps % SCAN_UNROLL == 0:
        carry = lax.fori_loop(0, steps, step, carry, unroll=SCAN_UNROLL)
    else:
        for t in range(steps):
            carry = step(t, carry)
    cr_sc[...] = carry[0]
    ci_sc[...] = carry[1]
    y = _dot(buf[...].astype(BF16), c_ref[0]) + d_ref[...] * u
    y_ref[...] = jax.nn.gelu(y).astype(BF16)

    @pl.when(c == n_chunks - 1)
    def _():
        hr_ref[...] = carry[0]
        hi_ref[...] = carry[1]


def _s5_scan(u_tm, a_blk, b_blk, c_blk, d_skip, h0_re, h0_im, *, rows_per_step, steps):
    m, d = u_tm.shape
    rows = rows_per_step * steps
    n_chunks = m // rows
    ns = S5_BLOCK_STATE
    st = pl.BlockSpec((rows_per_step, ns), lambda k, c: (0, k))
    st_shape = jax.ShapeDtypeStruct((rows_per_step, S5_BLOCKS * ns), F32)
    return pl.pallas_call(
        functools.partial(_s5_scan_kernel, rows_per_step=rows_per_step, steps=steps, n_chunks=n_chunks),
        grid=(S5_BLOCKS, n_chunks),
        in_specs=[pl.BlockSpec((rows, S5_BLOCK_IN), lambda k, c: (c, k)),
                  pl.BlockSpec((1, 1, 2 * ns), lambda k, c: (k, 0, 0)),
                  pl.BlockSpec((1, S5_BLOCK_IN, 2 * ns), lambda k, c: (k, 0, 0)),
                  pl.BlockSpec((1, 2 * ns, S5_BLOCK_IN), lambda k, c: (k, 0, 0)),
                  pl.BlockSpec((1, S5_BLOCK_IN), lambda k, c: (0, k)),
                  st, st],
        out_specs=[pl.BlockSpec((rows, S5_BLOCK_IN), lambda k, c: (c, k)), st, st],
        out_shape=[jax.ShapeDtypeStruct((m, d), BF16), st_shape, st_shape],
        scratch_shapes=[pltpu.VMEM((rows, 2 * ns), F32), pltpu.VMEM((rows_per_step, ns), F32),
                        pltpu.VMEM((rows_per_step, ns), F32)],
        compiler_params=_cparams("arbitrary", "arbitrary"),
        name="s5_scan",
    )(u_tm, a_blk, b_blk, c_blk, d_skip.reshape(1, d), h0_re, h0_im)


def _glu_kernel(y_ref, w_ref, h_ref, g_ref, o_ref):
    d = o_ref.shape[1]
    y = y_ref[...]
    z1 = _dot(y, w_ref[:, :d])
    z2 = _dot(y, w_ref[:, d:])
    o_ref[...] = h_ref[...] + _rms_rows(z1 * jax.nn.sigmoid(z2), g_ref[...])


def _glu_out(y, w_bf, h, g, *, tm):
    m, d = h.shape
    row = lambda n: pl.BlockSpec((tm, n), lambda i: (i, 0))
    return pl.pallas_call(
        _glu_kernel,
        grid=(m // tm,),
        in_specs=[row(d), _const_spec(w_bf.shape), row(d), _const_spec((1, d))],
        out_specs=row(d),
        out_shape=jax.ShapeDtypeStruct((m, d), F32),
        compiler_params=_cparams("arbitrary"),
        name="glu_out",
    )(y, w_bf, h, g.reshape(1, d))


def _rope_tables(pos):
    half = QK_ROPE // 2
    inv = ROPE_THETA ** (-jnp.arange(half, dtype=F32) / half)
    ang = pos.astype(F32)[:, None] * inv[None, :]
    reps = ROPE_LANES // half
    return jnp.tile(jnp.cos(ang), (1, reps)), jnp.tile(jnp.sin(ang), (1, reps))


def _rot_cols(w):
    half = QK_ROPE // 2
    return jnp.concatenate([-w[..., half:], w[..., :half]], axis=-1)


def _mla_weights(w_dq, w_uq, w_dkv, w_uk, w_uv, w_o):
    r = w_uq.shape[0]
    wr = w_uq[..., QK_NOPE:]
    wuq = jnp.concatenate([w_uq[..., :QK_NOPE].reshape(r, -1), wr.reshape(r, -1), _rot_cols(wr).reshape(r, -1)],
                          axis=1).astype(BF16)
    wk = w_dkv[:, KV_LORA:]
    wkr = _rot_cols(wk)
    wcat = jnp.concatenate([w_dq, w_dkv[:, :KV_LORA], wk, wk, wkr, wkr], axis=1).astype(BF16)
    wuk_h = w_uk.transpose(1, 0, 2).astype(BF16)
    wuk_t = w_uk.transpose(1, 2, 0).astype(BF16)
    wuv_h = w_uv.transpose(1, 0, 2).astype(BF16)
    return wcat, wuq, wuk_h, wuk_t, wuv_h, w_o.astype(BF16)


def _s5_block_mats(a_re, a_im, bb_re, bb_im, c_re, c_im):
    nb, gb, ng, ns = S5_BLOCKS, S5_BLOCK_GROUPS, S5_GROUP, S5_STATE
    eye = jnp.eye(gb, dtype=F32)
    bb = jnp.stack([bb_re, bb_im])
    bb = bb.reshape(2, ng, nb, gb, ns).transpose(2, 3, 1, 0, 4)
    b_blk = (bb[:, :, :, :, None, :] * eye[None, :, None, None, :, None]).reshape(nb, gb * ng, 2 * gb * ns)
    cc = jnp.stack([c_re, -c_im])
    cc = cc.reshape(2, nb, gb, ng, ns).transpose(1, 0, 2, 4, 3)
    c_blk = (cc[:, :, :, :, None, :] * eye[None, None, :, None, :, None]).reshape(nb, 2 * gb * ns, gb * ng)
    a_blk = jnp.concatenate([a_re.reshape(nb, gb * ns), a_im.reshape(nb, gb * ns)], axis=1)
    return a_blk.reshape(nb, 1, 2 * gb * ns), b_blk.astype(BF16), c_blk.astype(BF16)


PROMPT_ROWS_PER_STEP = 8
PROMPT_SCAN_STEPS = 128


def kernel(x_prompt, x_sample, cache_latent, cache_krope, page_table, state_pool, state_s5_re, state_s5_im,
           norm_gains, ffn_w_gate, ffn_w_up, ffn_w_down,
           mla_w_dq, mla_g_q, mla_w_uq, mla_w_dkv, mla_g_kv, mla_w_uk, mla_w_uv, mla_w_o,
           pool_w, pool_scale,
           s5_lam_re, s5_lam_im, s5_log_dt, s5_b_re, s5_b_im, s5_c_re, s5_c_im, s5_d, s5_w_glu):
    bp, tp, d = x_prompt.shape
    bs, ts, _ = x_sample.shape
    hp = x_prompt.reshape(bp * tp, d)
    hs = x_sample.reshape(bs * ts, d)
    cos_p, sin_p = _rope_tables(jnp.arange(tp))
    cos_s, sin_s = _rope_tables(PAST_LEN + jnp.arange(ts))
    cos_s, sin_s = jnp.tile(cos_s, (bs, 1)), jnp.tile(sin_s, (bs, 1))
    lat_p, kr_p, lat_s, kr_s = [], [], [], []
    pool_p, pool_s = [], []
    s5r_p, s5i_p, s5r_s, s5i_s = [], [], [], []
    tm_p, tm_s = 1024, bs * ts
    rows_tile = 512
    rows_tile_s = min(rows_tile, bs * ts)

    def ffn(h, g_pre, g_post, i, k, tm):
        return _half_ffn(h, g_pre, g_post, ffn_w_gate[i, k], ffn_w_up[i, k], ffn_w_down[i, k], tm=tm, tf=256)

    for i in range(DEPTH):
        g = norm_gains[i]
        j = i // N_MIXERS
        kind = i % N_MIXERS
        hp = ffn(hp, g[0], g[1], i, 0, tm_p)
        hs = ffn(hs, g[0], g[1], i, 0, tm_s)
        if kind == 0:
            wcat, wuq, wuk_h, wuk_t, wuv_h, wo = _mla_weights(
                mla_w_dq[j], mla_w_uq[j], mla_w_dkv[j], mla_w_uk[j], mla_w_uv[j], mla_w_o[j])
            proj = functools.partial(_mla_proj, g2=g[2], g_q=mla_g_q[j], g_kv=mla_g_kv[j], wcat=wcat, wuq=wuq)
            qn, qr, lat, latb, kr, kr2 = proj(hp, cos_t=cos_p, sin_t=sin_p, tm=rows_tile)
            att = _prompt_attn(qn, qr, latb, kr2, wuk_h, wuv_h, b=bp, t=tp, tq=256)
            hp = _out_proj(att, wo, hp, g[3], tm=rows_tile)
            lat_p.append(lat.reshape(bp, tp, KV_LORA))
            kr_p.append(kr.reshape(bp, tp, QK_ROPE))

            qn, qr, lat, _, kr, _ = proj(hs, cos_t=cos_s, sin_t=sin_s, tm=rows_tile_s)
            ql = _absorb_q(qn, wuk_t)
            ql = ql.reshape(N_HEADS, bs, ts, KV_LORA).transpose(1, 0, 2, 3).reshape(bs, N_HEADS * ts, KV_LORA)
            qrr = qr.reshape(bs, ts, N_HEADS, QK_ROPE).transpose(0, 2, 1, 3).reshape(bs, N_HEADS * ts, QK_ROPE)
            lat = lat.reshape(bs, ts, KV_LORA)
            kr = kr.reshape(bs, ts, QK_ROPE)
            o_lat = _sample_attn(ql, qrr, lat, kr, cache_latent, cache_krope, page_table, j)
            o_lat = o_lat.reshape(bs, N_HEADS, ts, KV_LORA).transpose(1, 0, 2, 3).reshape(N_HEADS, bs * ts, KV_LORA)
            hs = _out_proj(_expand_o(o_lat, wuv_h), wo, hs, g[3], tm=rows_tile_s)
            lat_s.append(lat)
            kr_s.append(kr)
        elif kind == 1:
            wp = pool_w[j].astype(BF16)
            hp, st = _pool_prompt(hp, g[2], g[3], pool_scale[j], wp, b=bp, t=tp, tm=rows_tile)
            pool_p.append(st)
            h_tm = hs.reshape(bs, ts, d).transpose(1, 0, 2)
            h_tm, st_tm = _pool_sample(state_pool[j].transpose(1, 0, 2), h_tm, g[2], g[3], pool_scale[j], wp,
                                       nb=min(32, bs), start_pos=PAST_LEN)
            hs = h_tm.transpose(1, 0, 2).reshape(bs * ts, d)
            pool_s.append(st_tm.transpose(1, 0, 2))
        else:
            a_re, a_im, bb_re, bb_im = _s5_params(s5_lam_re[j], s5_lam_im[j], s5_log_dt[j],
                                                  s5_b_re[j].transpose(2, 0, 1), s5_b_im[j].transpose(2, 0, 1))
            a_blk, b_blk, c_blk = _s5_block_mats(a_re, a_im, bb_re, bb_im, s5_c_re[j], s5_c_im[j])
            wglu = s5_w_glu[j].astype(BF16)
            n_state = S5_GROUPS * S5_STATE

            rp = PROMPT_ROWS_PER_STEP
            h_tm = jnp.pad(hp.reshape(bp, tp, d).transpose(1, 0, 2), ((0, 0), (0, rp - bp), (0, 0))).reshape(tp * rp, d)
            zero_state = jnp.zeros((rp, n_state), F32)
            y, sr, si = _s5_scan(_rms_call(h_tm, g[2], tm=1024), a_blk, b_blk, c_blk, s5_d[j], zero_state, zero_state,
                                 rows_per_step=rp, steps=PROMPT_SCAN_STEPS)
            h_tm = _glu_out(y, wglu, h_tm, g[3], tm=rows_tile)
            hp = h_tm.reshape(tp, rp, d)[:, :bp].transpose(1, 0, 2).reshape(bp * tp, d)
            s5r_p.append(sr[:bp].reshape(bp, S5_GROUPS, S5_STATE))
            s5i_p.append(si[:bp].reshape(bp, S5_GROUPS, S5_STATE))

            h_tm = hs.reshape(bs, ts, d).transpose(1, 0, 2).reshape(ts * bs, d)
            y, sr, si = _s5_scan(_rms_call(h_tm, g[2], tm=rows_tile_s), a_blk, b_blk, c_blk, s5_d[j],
                                 state_s5_re[j].reshape(bs, n_state), state_s5_im[j].reshape(bs, n_state),
                                 rows_per_step=bs, steps=ts)
            h_tm = _glu_out(y, wglu, h_tm, g[3], tm=rows_tile_s)
            hs = h_tm.reshape(ts, bs, d).transpose(1, 0, 2).reshape(bs * ts, d)
            s5r_s.append(sr.reshape(bs, S5_GROUPS, S5_STATE))
            s5i_s.append(si.reshape(bs, S5_GROUPS, S5_STATE))
        hp = ffn(hp, g[4], g[5], i, 1, tm_p)
        hs = ffn(hs, g[4], g[5], i, 1, tm_s)
    return (hp.reshape(bp, tp, d), hs.reshape(bs, ts, d),
            jnp.stack(lat_p), jnp.stack(kr_p), jnp.stack(lat_s), jnp.stack(kr_s),
            jnp.stack(pool_p), jnp.stack(pool_s),
            jnp.stack(s5r_p), jnp.stack(s5i_p), jnp.stack(s5r_s), jnp.stack(s5i_s))
```

```python
import functools
import math

import jax
import jax.numpy as jnp
from jax import lax
from jax.experimental import pallas as pl
from jax.experimental.pallas import tpu as pltpu

D_MODEL = 2048
DEPTH = 4
PAST_LEN = 8192
PAGE_SIZE = 128
N_MIXERS = 3
D_FF = 5632
MACARON_WEIGHT = 0.5
NORM_EPS = 1e-6
N_HEADS = 16
Q_LORA = 512
KV_LORA = 512
QK_NOPE = 128
QK_ROPE = 64
V_DIM = 128
ROPE_THETA = 10000.0
ATTN_SCALE = 1.0 / math.sqrt(QK_NOPE + QK_ROPE)
NEG_INF = -1e30
POOL_WINDOWS = (2, 4, 8, 16)
POOL_GROUP = D_MODEL // len(POOL_WINDOWS)
POOL_HIST = max(POOL_WINDOWS) - 1
S5_GROUP = 16
S5_GROUPS = D_MODEL // S5_GROUP
S5_STATE = 64

F32 = jnp.float32
BF16 = jnp.bfloat16

V7X_VMEM_BYTES = 64 * 1024 * 1024
VMEM_LIMIT_BYTES = V7X_VMEM_BYTES - 8 * 1024 * 1024

DOWN_COLS = 512
NORM_ROWS = 128


def _rms_rows(x, gain):
    ms = jnp.mean(x * x, axis=-1, keepdims=True)
    return x * lax.rsqrt(ms + NORM_EPS) * gain


def _cparams(*sem):
    return pltpu.CompilerParams(dimension_semantics=sem, vmem_limit_bytes=VMEM_LIMIT_BYTES)


def _const_spec(shape):
    nd = len(shape)
    return pl.BlockSpec(shape, lambda *_: (0,) * nd, pipeline_mode=pl.Buffered(1))


def _dot(a, b):
    return jnp.dot(a, b, preferred_element_type=F32)


def _dot_nt(a, b):
    return lax.dot_general(a, b, (((1,), (1,)), ((), ())), preferred_element_type=F32)


def _ffn_kernel(h_ref, gpre_ref, gpost_ref, wg_ref, wu_ref, wd_ref, o_ref, u_ref, *, n_f, tm):
    f = pl.program_id(1)
    rows = min(NORM_ROWS, tm)
    n_chunks = tm // rows

    @pl.when(f == 0)
    def _():
        def body(c, carry):
            r = pl.multiple_of(c * rows, rows)
            x = h_ref[pl.ds(r, rows), :]
            u_ref[pl.ds(r, rows), :] = _rms_rows(x, gpre_ref[...]).astype(BF16)
            o_ref[pl.ds(r, rows), :] = jnp.zeros((rows, o_ref.shape[1]), F32)
            return carry
        lax.fori_loop(0, n_chunks, body, 0)

    u = u_ref[...]
    gate = _dot(u, wg_ref[...].astype(BF16))
    up = _dot(u, wu_ref[...].astype(BF16))
    act = (gate * jax.nn.sigmoid(gate) * up).astype(BF16)
    wd = wd_ref[...].astype(BF16)
    d = o_ref.shape[1]
    for c in range(d // DOWN_COLS):
        cols = slice(c * DOWN_COLS, (c + 1) * DOWN_COLS)
        o_ref[:, cols] += _dot(act, wd[:, cols])

    @pl.when(f == n_f - 1)
    def _():
        def body(c, carry):
            r = pl.multiple_of(c * rows, rows)
            y = o_ref[pl.ds(r, rows), :]
            o_ref[pl.ds(r, rows), :] = (
                h_ref[pl.ds(r, rows), :] + MACARON_WEIGHT * _rms_rows(y, gpost_ref[...]))
            return carry
        lax.fori_loop(0, n_chunks, body, 0)


def _half_ffn(h, g_pre, g_post, w_gate, w_up, w_down, *, layer, half, tm, tf):
    m, d = h.shape
    n_f = D_FF // tf
    return pl.pallas_call(
        functools.partial(_ffn_kernel, n_f=n_f, tm=tm),
        grid=(m // tm, n_f),
        in_specs=[
            pl.BlockSpec((tm, d), lambda i, f: (i, 0)),
            pl.BlockSpec((1, d), lambda i, f: (0, 0)),
            pl.BlockSpec((1, d), lambda i, f: (0, 0)),
            pl.BlockSpec((None, None, d, tf), lambda i, f: (layer, half, 0, f)),
            pl.BlockSpec((None, None, d, tf), lambda i, f: (layer, half, 0, f)),
            pl.BlockSpec((None, None, tf, d), lambda i, f: (layer, half, f, 0)),
        ],
        out_specs=pl.BlockSpec((tm, d), lambda i, f: (i, 0)),
        out_shape=jax.ShapeDtypeStruct((m, d), F32),
        scratch_shapes=[pltpu.VMEM((tm, d), BF16)],
        compiler_params=_cparams("arbitrary", "arbitrary"),
        name="half_ffn",
    )(h, g_pre.reshape(1, d), g_post.reshape(1, d), w_gate, w_up, w_down)


ROPE_LANES = 128


def _mla_proj_kernel(h_ref, g2_ref, gq_ref, gkv_ref, wcat_ref, wuq_ref, cos_ref, sin_ref,
                     qn_ref, qr_ref, lat_ref, latb_ref, kr_ref, kr2_ref):
    u = _rms_rows(h_ref[...], g2_ref[...]).astype(BF16)
    r = _dot(u, wcat_ref[...])
    c = _rms_rows(r[:, :Q_LORA], gq_ref[...]).astype(BF16)
    lat = _rms_rows(r[:, Q_LORA:Q_LORA + KV_LORA], gkv_ref[...])
    lat_ref[...] = lat
    latb_ref[...] = lat.astype(BF16)
    cos = cos_ref[...]
    sin = sin_ref[...]
    k0 = Q_LORA + KV_LORA
    kr2 = r[:, k0:k0 + ROPE_LANES] * cos + r[:, k0 + ROPE_LANES:k0 + 2 * ROPE_LANES] * sin
    kr_ref[...] = kr2[:, :QK_ROPE]
    kr2_ref[...] = kr2.astype(BF16)
    q = _dot(c, wuq_ref[...])
    n_nope = N_HEADS * QK_NOPE
    n_rope = N_HEADS * QK_ROPE
    qn_ref[...] = q[:, :n_nope].astype(BF16)
    reps = n_rope // ROPE_LANES
    cos_q = jnp.concatenate([cos] * reps, axis=1)
    sin_q = jnp.concatenate([sin] * reps, axis=1)
    qr = q[:, n_nope:n_nope + n_rope] * cos_q + q[:, n_nope + n_rope:] * sin_q
    qr_ref[...] = qr.astype(BF16)


def _mla_proj(h, g2, g_q, g_kv, wcat, wuq, cos_t, sin_t, *, tm):
    m, d = h.shape
    tab_blocks = cos_t.shape[0] // tm
    n_nope = N_HEADS * QK_NOPE
    n_rope = N_HEADS * QK_ROPE
    row = lambda n: pl.BlockSpec((tm, n), lambda i: (i, 0))
    tab = pl.BlockSpec((tm, ROPE_LANES), lambda i: (i % tab_blocks, 0))
    return pl.pallas_call(
        _mla_proj_kernel,
        grid=(m // tm,),
        in_specs=[row(d), _const_spec((1, d)), _const_spec((1, Q_LORA)), _const_spec((1, KV_LORA)),
                  _const_spec(wcat.shape), _const_spec(wuq.shape), tab, tab],
        out_specs=[row(n_nope), row(n_rope), row(KV_LORA), row(KV_LORA), row(QK_ROPE), row(ROPE_LANES)],
        out_shape=[jax.ShapeDtypeStruct((m, n_nope), BF16), jax.ShapeDtypeStruct((m, n_rope), BF16),
                   jax.ShapeDtypeStruct((m, KV_LORA), F32), jax.ShapeDtypeStruct((m, KV_LORA), BF16),
                   jax.ShapeDtypeStruct((m, QK_ROPE), F32), jax.ShapeDtypeStruct((m, ROPE_LANES), BF16)],
        compiler_params=_cparams("arbitrary"),
        name="mla_proj",
    )(h, g2.reshape(1, d), g_q.reshape(1, Q_LORA), g_kv.reshape(1, KV_LORA), wcat, wuq, cos_t, sin_t)


def _prompt_attn_kernel(qn_ref, qr_ref, latb_ref, kr2_ref, wkv_ref, o_ref, k_sc, v_sc, *, tq):
    head = pl.program_id(1)
    t = latb_ref.shape[0]
    kv = _dot(latb_ref[...], wkv_ref[0])
    k_sc[:, :QK_NOPE] = kv[:, :QK_NOPE].astype(BF16)
    lane = lax.broadcasted_iota(jnp.int32, (t, ROPE_LANES), 1)
    mine = (lane // QK_ROPE) == (head % 2)
    k_sc[:, QK_NOPE:] = jnp.where(mine, kr2_ref[...], jnp.zeros_like(kr2_ref[...]))
    v_sc[...] = kv[:, QK_NOPE:].astype(BF16)

    rows = lax.broadcasted_iota(jnp.int32, (tq, tq), 0)
    cols = lax.broadcasted_iota(jnp.int32, (tq, tq), 1)
    causal = cols <= rows
    for qi in range(t // tq):
        own = slice(qi * tq, (qi + 1) * tq)
        past = slice(0, qi * tq)
        q = jnp.concatenate([qn_ref[own, :], qr_ref[own, :]], axis=1)
        s_own = jnp.where(causal, _dot_nt(q, k_sc[own, :]) * ATTN_SCALE, NEG_INF)
        m = jnp.max(s_own, axis=1, keepdims=True)
        if qi:
            s_past = _dot_nt(q, k_sc[past, :]) * ATTN_SCALE
            m = jnp.maximum(m, jnp.max(s_past, axis=1, keepdims=True))
        p_own = jnp.exp(s_own - m)
        l = jnp.sum(p_own, axis=1, keepdims=True)
        acc = _dot(p_own.astype(BF16), v_sc[own, :])
        if qi:
            p_past = jnp.exp(s_past - m)
            l = l + jnp.sum(p_past, axis=1, keepdims=True)
            acc = acc + _dot(p_past.astype(BF16), v_sc[past, :])
        o_ref[own, :] = (acc / l).astype(BF16)


def _prompt_attn(qn, qr, latb, kr2, wkv_h, *, b, t, tq):
    return pl.pallas_call(
        functools.partial(_prompt_attn_kernel, tq=tq),
        grid=(b, N_HEADS),
        in_specs=[
            pl.BlockSpec((t, QK_NOPE), lambda bi, h: (bi, h)),
            pl.BlockSpec((t, ROPE_LANES), lambda bi, h: (bi, h // 2)),
            pl.BlockSpec((t, KV_LORA), lambda bi, h: (bi, 0)),
            pl.BlockSpec((t, ROPE_LANES), lambda bi, h: (bi, 0)),
            pl.BlockSpec((1, KV_LORA, QK_NOPE + V_DIM), lambda bi, h: (h, 0, 0)),
        ],
        out_specs=pl.BlockSpec((t, V_DIM), lambda bi, h: (bi, h)),
        out_shape=jax.ShapeDtypeStruct((b * t, N_HEADS * V_DIM), BF16),
        scratch_shapes=[pltpu.VMEM((t, QK_NOPE + ROPE_LANES), BF16), pltpu.VMEM((t, V_DIM), BF16)],
        compiler_params=_cparams("arbitrary", "arbitrary"),
        name="prompt_attn",
    )(qn, qr, latb, kr2, wkv_h)


def _bmm_kernel(x_ref, w_ref, o_ref):
    o_ref[...] = _dot(x_ref[...], w_ref[...]).astype(o_ref.dtype)


def _absorb_q(qn, wuk_t):
    m = qn.shape[0]
    return pl.pallas_call(
        _bmm_kernel,
        grid=(N_HEADS,),
        in_specs=[pl.BlockSpec((m, QK_NOPE), lambda h: (0, h)),
                  pl.BlockSpec((None, QK_NOPE, KV_LORA), lambda h: (h, 0, 0))],
        out_specs=pl.BlockSpec((None, m, KV_LORA), lambda h: (h, 0, 0)),
        out_shape=jax.ShapeDtypeStruct((N_HEADS, m, KV_LORA), BF16),
        compiler_params=_cparams("arbitrary"),
        name="absorb_q",
    )(qn, wuk_t)


def _expand_o(o_lat, wuv_h):
    m = o_lat.shape[1]
    return pl.pallas_call(
        _bmm_kernel,
        grid=(N_HEADS,),
        in_specs=[pl.BlockSpec((None, m, KV_LORA), lambda h: (h, 0, 0)),
                  pl.BlockSpec((None, KV_LORA, V_DIM), lambda h: (h, 0, 0))],
        out_specs=pl.BlockSpec((m, V_DIM), lambda h: (0, h)),
        out_shape=jax.ShapeDtypeStruct((m, N_HEADS * V_DIM), BF16),
        compiler_params=_cparams("arbitrary"),
        name="expand_o",
    )(o_lat, wuv_h)


PAGES_PER_STEP = 16


def _sample_attn_kernel(pt_ref, ql_ref, qr_ref, latn_ref, krn_ref, *rest, n_steps, t_new):
    del pt_ref
    lat_refs = rest[:PAGES_PER_STEP]
    kr_refs = rest[PAGES_PER_STEP:2 * PAGES_PER_STEP]
    o_ref, m_sc, l_sc, acc_sc = rest[2 * PAGES_PER_STEP:]
    j = pl.program_id(1)
    rows = ql_ref.shape[1]

    @pl.when(j == 0)
    def _():
        m_sc[...] = jnp.full(m_sc.shape, NEG_INF, F32)
        l_sc[...] = jnp.zeros(l_sc.shape, F32)
        acc_sc[...] = jnp.zeros(acc_sc.shape, F32)

    ql = ql_ref[0]
    qr = qr_ref[0]
    lats = [lat_refs[i][0, 0].astype(BF16) for i in range(PAGES_PER_STEP)]
    s = jnp.concatenate(
        [_dot_nt(ql, lats[i]) + _dot(qr, kr_refs[i][0, 0].astype(BF16)) for i in range(PAGES_PER_STEP)],
        axis=1) * ATTN_SCALE
    m_prev = m_sc[...]
    m_new = jnp.maximum(m_prev, jnp.max(s, axis=1, keepdims=True))
    alpha = jnp.exp(m_prev - m_new)
    p = jnp.exp(s - m_new)
    l_new = alpha * l_sc[...] + jnp.sum(p, axis=1, keepdims=True)
    pb = p.astype(BF16)
    pv = _dot(pb[:, :PAGE_SIZE], lats[0])
    for i in range(1, PAGES_PER_STEP):
        pv += _dot(pb[:, i * PAGE_SIZE:(i + 1) * PAGE_SIZE], lats[i])
    acc_new = alpha * acc_sc[...] + pv
    m_sc[...] = m_new
    l_sc[...] = l_new
    acc_sc[...] = acc_new

    @pl.when(j == n_steps - 1)
    def _():
        latn = latn_ref[0].astype(BF16).astype(F32)
        krn = krn_ref[0].astype(BF16).astype(F32)
        qlf = ql.astype(F32)
        qrf = qr.astype(F32)
        tok = lax.broadcasted_iota(jnp.int32, (rows, 1), 0) % t_new
        s_new = []
        for k in range(t_new):
            sk = (jnp.sum(qlf * latn[k:k + 1, :], axis=1, keepdims=True)
                  + jnp.sum(qrf * krn[k:k + 1, :], axis=1, keepdims=True)) * ATTN_SCALE
            s_new.append(jnp.where(k <= tok, sk, NEG_INF))
        m_fin = m_new
        for sk in s_new:
            m_fin = jnp.maximum(m_fin, sk)
        a_fin = jnp.exp(m_new - m_fin)
        l_fin = a_fin * l_new
        acc = a_fin * acc_new
        for k in range(t_new):
            pk = jnp.exp(s_new[k] - m_fin)
            l_fin = l_fin + pk
            acc = acc + pk.astype(BF16).astype(F32) * latn[k:k + 1, :]
        o_ref[0] = (acc / l_fin).astype(BF16)


def _sample_attn(ql, qr, lat_new, kr_new, cache_latent, cache_krope_t, page_table, layer):
    b, rows, _ = ql.shape
    t_new = lat_new.shape[1]
    n_steps = page_table.shape[1] // PAGES_PER_STEP

    def page(i):
        return lambda bi, j, pt: (layer, pt[bi, j * PAGES_PER_STEP + i], 0, 0)

    per_seq = lambda n, w: pl.BlockSpec((1, n, w), lambda bi, j, pt: (bi, 0, 0))
    grid_spec = pltpu.PrefetchScalarGridSpec(
        num_scalar_prefetch=1,
        grid=(b, n_steps),
        in_specs=[per_seq(rows, KV_LORA), per_seq(rows, QK_ROPE), per_seq(t_new, KV_LORA), per_seq(t_new, QK_ROPE)]
        + [pl.BlockSpec((1, 1, PAGE_SIZE, KV_LORA), page(i)) for i in range(PAGES_PER_STEP)]
        + [pl.BlockSpec((1, 1, QK_ROPE, PAGE_SIZE), page(i)) for i in range(PAGES_PER_STEP)],
        out_specs=per_seq(rows, KV_LORA),
        scratch_shapes=[pltpu.VMEM((rows, 1), F32), pltpu.VMEM((rows, 1), F32), pltpu.VMEM((rows, KV_LORA), F32)],
    )
    return pl.pallas_call(
        functools.partial(_sample_attn_kernel, n_steps=n_steps, t_new=t_new),
        grid_spec=grid_spec,
        out_shape=jax.ShapeDtypeStruct((b, rows, KV_LORA), BF16),
        compiler_params=_cparams("arbitrary", "arbitrary"),
        name="sample_attn",
    )(page_table, ql, qr, lat_new, kr_new, *([cache_latent] * PAGES_PER_STEP), *([cache_krope_t] * PAGES_PER_STEP))


def _out_proj_kernel(x_ref, w_ref, h_ref, g_ref, o_ref):
    y = _dot(x_ref[...], w_ref[...])
    o_ref[...] = h_ref[...] + _rms_rows(y, g_ref[...])


def _out_proj(x, w, h, g, *, tm):
    m, d = h.shape
    row = lambda n: pl.BlockSpec((tm, n), lambda i: (i, 0))
    return pl.pallas_call(
        _out_proj_kernel,
        grid=(m // tm,),
        in_specs=[row(x.shape[1]), _const_spec(w.shape), row(d), _const_spec((1, d))],
        out_specs=row(d),
        out_shape=jax.ShapeDtypeStruct((m, d), F32),
        compiler_params=_cparams("arbitrary"),
        name="out_proj",
    )(x, w, h, g.reshape(1, d))


HIST_ROWS = 16


def _pool_mix_cols(window_sum, u_cols, cnt, w_g, scale_cols):
    pooled = window_sum / cnt - u_cols
    return _dot(pooled.astype(BF16), w_g) * scale_cols


def _pool_prompt_kernel(h_ref, g2_ref, g3_ref, scale_ref, w_ref, o_ref, st_ref, ubuf, *, tm, n_t):
    ti = pl.program_id(1)

    @pl.when(ti == 0)
    def _():
        ubuf[0:HIST_ROWS, :] = jnp.zeros((HIST_ROWS, ubuf.shape[1]), F32)

    @pl.when(ti > 0)
    def _():
        ubuf[0:HIST_ROWS, :] = ubuf[tm:tm + HIST_ROWS, :]

    h = h_ref[...]
    ubuf[HIST_ROWS:HIST_ROWS + tm, :] = _rms_rows(h, g2_ref[...])
    pos = ti * tm + lax.broadcasted_iota(jnp.int32, (tm, 1), 0)
    for g, w in enumerate(POOL_WINDOWS):
        cols = slice(g * POOL_GROUP, (g + 1) * POOL_GROUP)
        u_cols = ubuf[HIST_ROWS:HIST_ROWS + tm, cols]
        win = u_cols
        for k in range(1, w):
            win = win + ubuf[HIST_ROWS - k:HIST_ROWS - k + tm, cols]
        cnt = jnp.minimum(pos + 1, w).astype(F32)
        o_ref[:, cols] = _pool_mix_cols(win, u_cols, cnt, w_ref[g], scale_ref[:, cols])
    o_ref[...] = h + _rms_rows(o_ref[...], g3_ref[...])

    @pl.when(ti == n_t - 1)
    def _():
        st_ref[0] = ubuf[HIST_ROWS + tm - POOL_HIST:HIST_ROWS + tm, :]


def _pool_prompt(h, g2, g3, scale, w_bf, *, b, t, tm):
    d = h.shape[1]
    n_t = t // tm
    return pl.pallas_call(
        functools.partial(_pool_prompt_kernel, tm=tm, n_t=n_t),
        grid=(b, n_t),
        in_specs=[pl.BlockSpec((tm, d), lambda bi, ti: (bi * n_t + ti, 0)),
                  _const_spec((1, d)), _const_spec((1, d)), _const_spec((1, d)), _const_spec(w_bf.shape)],
        out_specs=[pl.BlockSpec((tm, d), lambda bi, ti: (bi * n_t + ti, 0)),
                   pl.BlockSpec((1, POOL_HIST, d), lambda bi, ti: (bi, 0, 0))],
        out_shape=[jax.ShapeDtypeStruct(h.shape, F32), jax.ShapeDtypeStruct((b, POOL_HIST, d), F32)],
        scratch_shapes=[pltpu.VMEM((HIST_ROWS + tm, d), F32)],
        compiler_params=_cparams("arbitrary", "arbitrary"),
        name="pool_prompt",
    )(h, g2.reshape(1, d), g3.reshape(1, d), scale.reshape(1, d), w_bf)


def _pool_sample_kernel(hist_ref, h_ref, g2_ref, g3_ref, scale_ref, w_ref, o_ref, st_ref, mix_sc, *, start_pos):
    t_new = h_ref.shape[0]
    nb = h_ref.shape[1]
    hs = [h_ref[t] for t in range(t_new)]
    ext = [hist_ref[k] for k in range(POOL_HIST)] + [_rms_rows(x, g2_ref[...]) for x in hs]
    for t in range(t_new):
        for g, w in enumerate(POOL_WINDOWS):
            cols = slice(g * POOL_GROUP, (g + 1) * POOL_GROUP)
            u_cols = ext[POOL_HIST + t][:, cols]
            win = u_cols
            for k in range(1, w):
                win = win + ext[POOL_HIST + t - k][:, cols]
            cnt = float(min(start_pos + t + 1, w))
            mix_sc[t * nb:(t + 1) * nb, cols] = _pool_mix_cols(win, u_cols, cnt, w_ref[g], scale_ref[:, cols])
    for t in range(t_new):
        o_ref[t] = hs[t] + _rms_rows(mix_sc[t * nb:(t + 1) * nb, :], g3_ref[...])
    for k in range(POOL_HIST):
        st_ref[k] = ext[t_new + k]


def _pool_sample(hist_tm, h_tm, g2, g3, scale, w_bf, *, nb, start_pos):
    t_new, b, d = h_tm.shape
    seq = lambda n: pl.BlockSpec((n, nb, d), lambda i: (0, i, 0))
    return pl.pallas_call(
        functools.partial(_pool_sample_kernel, start_pos=start_pos),
        grid=(b // nb,),
        in_specs=[seq(POOL_HIST), seq(t_new), _const_spec((1, d)), _const_spec((1, d)), _const_spec((1, d)),
                  _const_spec(w_bf.shape)],
        out_specs=[seq(t_new), seq(POOL_HIST)],
        out_shape=[jax.ShapeDtypeStruct(h_tm.shape, F32), jax.ShapeDtypeStruct(hist_tm.shape, F32)],
        scratch_shapes=[pltpu.VMEM((t_new * nb, d), F32)],
        compiler_params=_cparams("arbitrary"),
        name="pool_sample",
    )(hist_tm, h_tm, g2.reshape(1, d), g3.reshape(1, d), scale.reshape(1, d), w_bf)


def _s5_params_kernel(lr_ref, li_ref, logdt_ref, br_ref, bi_ref, ar_ref, ai_ref, bbr_ref, bbi_ref):
    lr = lr_ref[...]
    li = li_ref[...]
    dt = jnp.exp(logdt_ref[...])
    mag = jnp.exp(lr * dt)
    a_re = mag * jnp.cos(li * dt)
    a_im = mag * jnp.sin(li * dt)
    den = lr * lr + li * li
    nr = a_re - 1.0
    ni = a_im
    zoh_re = (nr * lr + ni * li) / den
    zoh_im = (ni * lr - nr * li) / den
    ar_ref[...] = a_re
    ai_ref[...] = a_im
    for c in range(S5_GROUP):
        b_re = br_ref[c]
        b_im = bi_ref[c]
        bbr_ref[c] = zoh_re * b_re - zoh_im * b_im
        bbi_ref[c] = zoh_re * b_im + zoh_im * b_re


def _s5_params(lam_re, lam_im, log_dt, b_re_cgp, b_im_cgp):
    g, p = lam_re.shape
    full = lambda shape: pl.BlockSpec(shape, lambda: (0,) * len(shape))
    gp = jax.ShapeDtypeStruct((g, p), F32)
    cgp = jax.ShapeDtypeStruct(b_re_cgp.shape, F32)
    return pl.pallas_call(
        _s5_params_kernel,
        in_specs=[full((g, p)), full((g, p)), full((g, 1)), full(b_re_cgp.shape), full(b_im_cgp.shape)],
        out_specs=[full((g, p)), full((g, p)), full(b_re_cgp.shape), full(b_re_cgp.shape)],
        out_shape=[gp, gp, cgp, cgp],
        name="s5_params",
    )(lam_re, lam_im, log_dt.reshape(g, 1), b_re_cgp, b_im_cgp)


def _rms_kernel(h_ref, g_ref, o_ref):
    o_ref[...] = _rms_rows(h_ref[...], g_ref[...])


def _rms_call(h, g, *, tm):
    m, d = h.shape
    row = pl.BlockSpec((tm, d), lambda i: (i, 0))
    return pl.pallas_call(
        _rms_kernel, grid=(m // tm,), in_specs=[row, _const_spec((1, d))], out_specs=row,
        out_shape=jax.ShapeDtypeStruct((m, d), F32), compiler_params=_cparams("arbitrary"), name="rms_rows",
    )(h, g.reshape(1, d))


S5_BLOCK_GROUPS = 16
S5_BLOCK_IN = S5_BLOCK_GROUPS * S5_GROUP
S5_BLOCK_STATE = S5_BLOCK_GROUPS * S5_STATE
S5_BLOCKS = S5_GROUPS // S5_BLOCK_GROUPS
SCAN_UNROLL = 8


def _s5_scan_kernel(u_ref, a_ref, b_ref, c_ref, d_ref, h0r_ref, h0i_ref, y_ref, hr_ref, hi_ref,
                    buf, cr_sc, ci_sc, *, rows_per_step, steps, n_chunks):
    c = pl.program_id(1)
    rps = rows_per_step
    ns = S5_BLOCK_STATE

    @pl.when(c == 0)
    def _():
        cr_sc[...] = h0r_ref[...]
        ci_sc[...] = h0i_ref[...]

    u = u_ref[...]
    buf[...] = _dot(u.astype(BF16), b_ref[0])
    a = a_ref[0]
    a_re = jnp.broadcast_to(a[:, :ns], (rps, ns))
    a_im = jnp.broadcast_to(a[:, ns:], (rps, ns))

    def step(t, carry):
        h_re, h_im = carry
        r0 = pl.multiple_of(t * rps, rps)
        n_re = a_re * h_re - a_im * h_im + buf[pl.ds(r0, rps), :ns]
        n_im = a_re * h_im + a_im * h_re + buf[pl.ds(r0, rps), ns:]
        buf[pl.ds(r0, rps), :ns] = n_re
        buf[pl.ds(r0, rps), ns:] = n_im
        return n_re, n_im

    carry = (cr_sc[...], ci_sc[...])
    if steps % SCAN_UNROLL == 0:
        carry = lax.fori_loop(0, steps, step, carry, unroll=SCAN_UNROLL)
    else:
        for t in range(steps):
            carry = step(t, carry)
    cr_sc[...] = carry[0]
    ci_sc[...] = carry[1]
    y = _dot(buf[...].astype(BF16), c_ref[0]) + d_ref[...] * u
    y_ref[...] = jax.nn.gelu(y).astype(BF16)

    @pl.when(c == n_chunks - 1)
    def _():
        hr_ref[...] = carry[0]
        hi_ref[...] = carry[1]


def _s5_scan(u_tm, a_blk, b_blk, c_blk, d_skip, h0_re, h0_im, *, rows_per_step, steps):
    m, d = u_tm.shape
    rows = rows_per_step * steps
    n_chunks = m // rows
    ns = S5_BLOCK_STATE
    st = pl.BlockSpec((rows_per_step, ns), lambda k, c: (0, k))
    st_shape = jax.ShapeDtypeStruct((rows_per_step, S5_BLOCKS * ns), F32)
    return pl.pallas_call(
        functools.partial(_s5_scan_kernel, rows_per_step=rows_per_step, steps=steps, n_chunks=n_chunks),
        grid=(S5_BLOCKS, n_chunks),
        in_specs=[pl.BlockSpec((rows, S5_BLOCK_IN), lambda k, c: (c, k)),
                  pl.BlockSpec((1, 1, 2 * ns), lambda k, c: (k, 0, 0)),
                  pl.BlockSpec((1, S5_BLOCK_IN, 2 * ns), lambda k, c: (k, 0, 0)),
                  pl.BlockSpec((1, 2 * ns, S5_BLOCK_IN), lambda k, c: (k, 0, 0)),
                  pl.BlockSpec((1, S5_BLOCK_IN), lambda k, c: (0, k)),
                  st, st],
        out_specs=[pl.BlockSpec((rows, S5_BLOCK_IN), lambda k, c: (c, k)), st, st],
        out_shape=[jax.ShapeDtypeStruct((m, d), BF16), st_shape, st_shape],
        scratch_shapes=[pltpu.VMEM((rows, 2 * ns), F32), pltpu.VMEM((rows_per_step, ns), F32),
                        pltpu.VMEM((rows_per_step, ns), F32)],
        compiler_params=_cparams("arbitrary", "arbitrary"),
        name="s5_scan",
    )(u_tm, a_blk, b_blk, c_blk, d_skip.reshape(1, d), h0_re, h0_im)


def _glu_kernel(y_ref, w_ref, h_ref, g_ref, o_ref):
    d = o_ref.shape[1]
    y = y_ref[...]
    z1 = _dot(y, w_ref[:, :d])
    z2 = _dot(y, w_ref[:, d:])
    o_ref[...] = h_ref[...] + _rms_rows(z1 * jax.nn.sigmoid(z2), g_ref[...])


def _glu_out(y, w_bf, h, g, *, tm):
    m, d = h.shape
    row = lambda n: pl.BlockSpec((tm, n), lambda i: (i, 0))
    return pl.pallas_call(
        _glu_kernel,
        grid=(m // tm,),
        in_specs=[row(d), _const_spec(w_bf.shape), row(d), _const_spec((1, d))],
        out_specs=row(d),
        out_shape=jax.ShapeDtypeStruct((m, d), F32),
        compiler_params=_cparams("arbitrary"),
        name="glu_out",
    )(y, w_bf, h, g.reshape(1, d))


def _rope_tables(pos):
    half = QK_ROPE // 2
    inv = ROPE_THETA ** (-jnp.arange(half, dtype=F32) / half)
    ang = pos.astype(F32)[:, None] * inv[None, :]
    reps = ROPE_LANES // half
    return jnp.tile(jnp.cos(ang), (1, reps)), jnp.tile(jnp.sin(ang), (1, reps))


def _rot_cols(w):
    half = QK_ROPE // 2
    return jnp.concatenate([-w[..., half:], w[..., :half]], axis=-1)


def _mla_weights(w_dq, w_uq, w_dkv, w_uk, w_uv, w_o):
    r = w_uq.shape[0]
    wr = w_uq[..., QK_NOPE:]
    wuq = jnp.concatenate([w_uq[..., :QK_NOPE].reshape(r, -1), wr.reshape(r, -1), _rot_cols(wr).reshape(r, -1)],
                          axis=1).astype(BF16)
    wk = w_dkv[:, KV_LORA:]
    wkr = _rot_cols(wk)
    wcat = jnp.concatenate([w_dq, w_dkv[:, :KV_LORA], wk, wk, wkr, wkr], axis=1).astype(BF16)
    wuk_t = w_uk.transpose(1, 2, 0).astype(BF16)
    wuv_h = w_uv.transpose(1, 0, 2).astype(BF16)
    wkv_h = jnp.concatenate([w_uk.transpose(1, 0, 2).astype(BF16), wuv_h], axis=2)
    return wcat, wuq, wkv_h, wuk_t, wuv_h, w_o.astype(BF16)


def _s5_block_mats(a_re, a_im, bb_re, bb_im, c_re, c_im):
    nb, gb, ng, ns = S5_BLOCKS, S5_BLOCK_GROUPS, S5_GROUP, S5_STATE
    eye = jnp.eye(gb, dtype=F32)
    bb = jnp.stack([bb_re, bb_im])
    bb = bb.reshape(2, ng, nb, gb, ns).transpose(2, 3, 1, 0, 4)
    b_blk = (bb[:, :, :, :, None, :] * eye[None, :, None, None, :, None]).reshape(nb, gb * ng, 2 * gb * ns)
    cc = jnp.stack([c_re, -c_im])
    cc = cc.reshape(2, nb, gb, ng, ns).transpose(1, 0, 2, 4, 3)
    c_blk = (cc[:, :, :, :, None, :] * eye[None, None, :, None, :, None]).reshape(nb, 2 * gb * ns, gb * ng)
    a_blk = jnp.concatenate([a_re.reshape(nb, gb * ns), a_im.reshape(nb, gb * ns)], axis=1)
    return a_blk.reshape(nb, 1, 2 * gb * ns), b_blk.astype(BF16), c_blk.astype(BF16)


PROMPT_ROWS_PER_STEP = 8
PROMPT_SCAN_STEPS = 128


def kernel(x_prompt, x_sample, cache_latent, cache_krope, page_table, state_pool, state_s5_re, state_s5_im,
           norm_gains, ffn_w_gate, ffn_w_up, ffn_w_down,
           mla_w_dq, mla_g_q, mla_w_uq, mla_w_dkv, mla_g_kv, mla_w_uk, mla_w_uv, mla_w_o,
           pool_w, pool_scale,
           s5_lam_re, s5_lam_im, s5_log_dt, s5_b_re, s5_b_im, s5_c_re, s5_c_im, s5_d, s5_w_glu):
    bp, tp, d = x_prompt.shape
    bs, ts, _ = x_sample.shape
    hp = x_prompt.reshape(bp * tp, d)
    hs = x_sample.reshape(bs * ts, d)
    cos_p, sin_p = _rope_tables(jnp.arange(tp))
    cos_s, sin_s = _rope_tables(PAST_LEN + jnp.arange(ts))
    cos_s, sin_s = jnp.tile(cos_s, (bs, 1)), jnp.tile(sin_s, (bs, 1))
    cache_krope_t = cache_krope.transpose(0, 1, 3, 2)
    lat_p, kr_p, lat_s, kr_s = [], [], [], []
    pool_p, pool_s = [], []
    s5r_p, s5i_p, s5r_s, s5i_s = [], [], [], []
    tm_p, tm_s = 1024, bs * ts
    rows_tile = 512
    rows_tile_s = min(rows_tile, bs * ts)

    def ffn(h, g_pre, g_post, i, k, tm):
        return _half_ffn(h, g_pre, g_post, ffn_w_gate, ffn_w_up, ffn_w_down, layer=i, half=k, tm=tm, tf=256)

    for i in range(DEPTH):
        g = norm_gains[i]
        j = i // N_MIXERS
        kind = i % N_MIXERS
        hp = ffn(hp, g[0], g[1], i, 0, tm_p)
        hs = ffn(hs, g[0], g[1], i, 0, tm_s)
        if kind == 0:
            wcat, wuq, wkv_h, wuk_t, wuv_h, wo = _mla_weights(
                mla_w_dq[j], mla_w_uq[j], mla_w_dkv[j], mla_w_uk[j], mla_w_uv[j], mla_w_o[j])
            proj = functools.partial(_mla_proj, g2=g[2], g_q=mla_g_q[j], g_kv=mla_g_kv[j], wcat=wcat, wuq=wuq)
            qn, qr, lat, latb, kr, kr2 = proj(hp, cos_t=cos_p, sin_t=sin_p, tm=rows_tile)
            att = _prompt_attn(qn, qr, latb, kr2, wkv_h, b=bp, t=tp, tq=min(512, tp))
            hp = _out_proj(att, wo, hp, g[3], tm=rows_tile)
            lat_p.append(lat.reshape(bp, tp, KV_LORA))
            kr_p.append(kr.reshape(bp, tp, QK_ROPE))

            qn, qr, lat, _, kr, _ = proj(hs, cos_t=cos_s, sin_t=sin_s, tm=rows_tile_s)
            ql = _absorb_q(qn, wuk_t)
            ql = ql.reshape(N_HEADS, bs, ts, KV_LORA).transpose(1, 0, 2, 3).reshape(bs, N_HEADS * ts, KV_LORA)
            qrr = qr.reshape(bs, ts, N_HEADS, QK_ROPE).transpose(0, 2, 1, 3).reshape(bs, N_HEADS * ts, QK_ROPE)
            lat = lat.reshape(bs, ts, KV_LORA)
            kr = kr.reshape(bs, ts, QK_ROPE)
            o_lat = _sample_attn(ql, qrr, lat, kr, cache_latent, cache_krope_t, page_table, j)
            o_lat = o_lat.reshape(bs, N_HEADS, ts, KV_LORA).transpose(1, 0, 2, 3).reshape(N_HEADS, bs * ts, KV_LORA)
            hs = _out_proj(_expand_o(o_lat, wuv_h), wo, hs, g[3], tm=rows_tile_s)
            lat_s.append(lat)
            kr_s.append(kr)
        elif kind == 1:
            wp = pool_w[j].astype(BF16)
            hp, st = _pool_prompt(hp, g[2], g[3], pool_scale[j], wp, b=bp, t=tp, tm=rows_tile)
            pool_p.append(st)
            h_tm = hs.reshape(bs, ts, d).transpose(1, 0, 2)
            h_tm, st_tm = _pool_sample(state_pool[j].transpose(1, 0, 2), h_tm, g[2], g[3], pool_scale[j], wp,
                                       nb=min(32, bs), start_pos=PAST_LEN)
            hs = h_tm.transpose(1, 0, 2).reshape(bs * ts, d)
            pool_s.append(st_tm.transpose(1, 0, 2))
        else:
            a_re, a_im, bb_re, bb_im = _s5_params(s5_lam_re[j], s5_lam_im[j], s5_log_dt[j],
                                                  s5_b_re[j].transpose(2, 0, 1), s5_b_im[j].transpose(2, 0, 1))
            a_blk, b_blk, c_blk = _s5_block_mats(a_re, a_im, bb_re, bb_im, s5_c_re[j], s5_c_im[j])
            wglu = s5_w_glu[j].astype(BF16)
            n_state = S5_GROUPS * S5_STATE

            rp = PROMPT_ROWS_PER_STEP
            h_tm = jnp.pad(hp.reshape(bp, tp, d).transpose(1, 0, 2), ((0, 0), (0, rp - bp), (0, 0))).reshape(tp * rp, d)
            zero_state = jnp.zeros((rp, n_state), F32)
            y, sr, si = _s5_scan(_rms_call(h_tm, g[2], tm=1024), a_blk, b_blk, c_blk, s5_d[j], zero_state, zero_state,
                                 rows_per_step=rp, steps=PROMPT_SCAN_STEPS)
            h_tm = _glu_out(y, wglu, h_tm, g[3], tm=rows_tile)
            hp = h_tm.reshape(tp, rp, d)[:, :bp].transpose(1, 0, 2).reshape(bp * tp, d)
            s5r_p.append(sr[:bp].reshape(bp, S5_GROUPS, S5_STATE))
            s5i_p.append(si[:bp].reshape(bp, S5_GROUPS, S5_STATE))

            h_tm = hs.reshape(bs, ts, d).transpose(1, 0, 2).reshape(ts * bs, d)
            y, sr, si = _s5_scan(_rms_call(h_tm, g[2], tm=rows_tile_s), a_blk, b_blk, c_blk, s5_d[j],
                                 state_s5_re[j].reshape(bs, n_state), state_s5_im[j].reshape(bs, n_state),
                                 rows_per_step=bs, steps=ts)
            h_tm = _glu_out(y, wglu, h_tm, g[3], tm=rows_tile_s)
            hs = h_tm.reshape(ts, bs, d).transpose(1, 0, 2).reshape(bs * ts, d)
            s5r_s.append(sr.reshape(bs, S5_GROUPS, S5_STATE))
            s5i_s.append(si.reshape(bs, S5_GROUPS, S5_STATE))
        hp = ffn(hp, g[4], g[5], i, 1, tm_p)
        hs = ffn(hs, g[4], g[5], i, 1, tm_s)
    return (hp.reshape(bp, tp, d), hs.reshape(bs, ts, d),
            jnp.stack(lat_p), jnp.stack(kr_p), jnp.stack(lat_s), jnp.stack(kr_s),
            jnp.stack(pool_p), jnp.stack(pool_s),
            jnp.stack(s5r_p), jnp.stack(s5i_p), jnp.stack(s5r_s), jnp.stack(s5i_s))
```

```python
import functools
import math

import jax
import jax.numpy as jnp
from jax import lax
from jax.experimental import pallas as pl
from jax.experimental.pallas import tpu as pltpu

D_MODEL = 2048
DEPTH = 4
PAST_LEN = 8192
PAGE_SIZE = 128
N_MIXERS = 3
D_FF = 5632
MACARON_WEIGHT = 0.5
NORM_EPS = 1e-6
N_HEADS = 16
Q_LORA = 512
KV_LORA = 512
QK_NOPE = 128
QK_ROPE = 64
V_DIM = 128
ROPE_THETA = 10000.0
ATTN_SCALE = 1.0 / math.sqrt(QK_NOPE + QK_ROPE)
NEG_INF = -1e30
POOL_WINDOWS = (2, 4, 8, 16)
POOL_GROUP = D_MODEL // len(POOL_WINDOWS)
POOL_HIST = max(POOL_WINDOWS) - 1
S5_GROUP = 16
S5_GROUPS = D_MODEL // S5_GROUP
S5_STATE = 64

F32 = jnp.float32
BF16 = jnp.bfloat16

V7X_VMEM_BYTES = 64 * 1024 * 1024
VMEM_LIMIT_BYTES = V7X_VMEM_BYTES - 8 * 1024 * 1024

DOWN_COLS = 512
NORM_ROWS = 128


def _rms_rows(x, gain):
    ms = jnp.mean(x * x, axis=-1, keepdims=True)
    return x * lax.rsqrt(ms + NORM_EPS) * gain


def _cparams(*sem):
    return pltpu.CompilerParams(dimension_semantics=sem, vmem_limit_bytes=VMEM_LIMIT_BYTES)


def _const_spec(shape):
    nd = len(shape)
    return pl.BlockSpec(shape, lambda *_: (0,) * nd, pipeline_mode=pl.Buffered(1))


def _dot(a, b):
    return jnp.dot(a, b, preferred_element_type=F32)


def _dot_nt(a, b):
    return lax.dot_general(a, b, (((1,), (1,)), ((), ())), preferred_element_type=F32)


def _ffn_kernel(h_ref, gpre_ref, gpost_ref, wg_ref, wu_ref, wd_ref, o_ref, u_ref, *, n_f, tm):
    f = pl.program_id(1)
    rows = min(NORM_ROWS, tm)
    n_chunks = tm // rows

    @pl.when(f == 0)
    def _():
        def body(c, carry):
            r = pl.multiple_of(c * rows, rows)
            x = h_ref[pl.ds(r, rows), :]
            u_ref[pl.ds(r, rows), :] = _rms_rows(x, gpre_ref[...]).astype(BF16)
            o_ref[pl.ds(r, rows), :] = jnp.zeros((rows, o_ref.shape[1]), F32)
            return carry
        lax.fori_loop(0, n_chunks, body, 0)

    u = u_ref[...]
    gate = _dot(u, wg_ref[...].astype(BF16))
    up = _dot(u, wu_ref[...].astype(BF16))
    act = (gate * jax.nn.sigmoid(gate) * up).astype(BF16)
    wd = wd_ref[...].astype(BF16)
    d = o_ref.shape[1]
    for c in range(d // DOWN_COLS):
        cols = slice(c * DOWN_COLS, (c + 1) * DOWN_COLS)
        o_ref[:, cols] += _dot(act, wd[:, cols])

    @pl.when(f == n_f - 1)
    def _():
        def body(c, carry):
            r = pl.multiple_of(c * rows, rows)
            y = o_ref[pl.ds(r, rows), :]
            o_ref[pl.ds(r, rows), :] = (
                h_ref[pl.ds(r, rows), :] + MACARON_WEIGHT * _rms_rows(y, gpost_ref[...]))
            return carry
        lax.fori_loop(0, n_chunks, body, 0)


def _half_ffn(h, g_pre, g_post, w_gate, w_up, w_down, *, layer, half, tm, tf):
    m, d = h.shape
    n_f = D_FF // tf
    return pl.pallas_call(
        functools.partial(_ffn_kernel, n_f=n_f, tm=tm),
        grid=(m // tm, n_f),
        in_specs=[
            pl.BlockSpec((tm, d), lambda i, f: (i, 0)),
            pl.BlockSpec((1, d), lambda i, f: (0, 0)),
            pl.BlockSpec((1, d), lambda i, f: (0, 0)),
            pl.BlockSpec((None, None, d, tf), lambda i, f: (layer, half, 0, f)),
            pl.BlockSpec((None, None, d, tf), lambda i, f: (layer, half, 0, f)),
            pl.BlockSpec((None, None, tf, d), lambda i, f: (layer, half, f, 0)),
        ],
        out_specs=pl.BlockSpec((tm, d), lambda i, f: (i, 0)),
        out_shape=jax.ShapeDtypeStruct((m, d), F32),
        scratch_shapes=[pltpu.VMEM((tm, d), BF16)],
        compiler_params=_cparams("arbitrary", "arbitrary"),
        name="half_ffn",
    )(h, g_pre.reshape(1, d), g_post.reshape(1, d), w_gate, w_up, w_down)


ROPE_LANES = 128


def _mla_proj_kernel(h_ref, g2_ref, gq_ref, gkv_ref, wcat_ref, wuq_ref, cos_ref, sin_ref,
                     qn_ref, qr_ref, lat_ref, latb_ref, kr_ref, kr2_ref):
    u = _rms_rows(h_ref[...], g2_ref[...]).astype(BF16)
    r = _dot(u, wcat_ref[...])
    c = _rms_rows(r[:, :Q_LORA], gq_ref[...]).astype(BF16)
    lat = _rms_rows(r[:, Q_LORA:Q_LORA + KV_LORA], gkv_ref[...])
    lat_ref[...] = lat
    latb_ref[...] = lat.astype(BF16)
    cos = cos_ref[...]
    sin = sin_ref[...]
    k0 = Q_LORA + KV_LORA
    kr2 = r[:, k0:k0 + ROPE_LANES] * cos + r[:, k0 + ROPE_LANES:k0 + 2 * ROPE_LANES] * sin
    kr_ref[...] = kr2[:, :QK_ROPE]
    kr2_ref[...] = kr2.astype(BF16)
    q = _dot(c, wuq_ref[...])
    n_nope = N_HEADS * QK_NOPE
    n_rope = N_HEADS * QK_ROPE
    qn_ref[...] = q[:, :n_nope].astype(BF16)
    reps = n_rope // ROPE_LANES
    cos_q = jnp.concatenate([cos] * reps, axis=1)
    sin_q = jnp.concatenate([sin] * reps, axis=1)
    qr = q[:, n_nope:n_nope + n_rope] * cos_q + q[:, n_nope + n_rope:] * sin_q
    qr_ref[...] = qr.astype(BF16)


def _mla_proj(h, g2, g_q, g_kv, wcat, wuq, cos_t, sin_t, *, tm):
    m, d = h.shape
    tab_blocks = cos_t.shape[0] // tm
    n_nope = N_HEADS * QK_NOPE
    n_rope = N_HEADS * QK_ROPE
    row = lambda n: pl.BlockSpec((tm, n), lambda i: (i, 0))
    tab = pl.BlockSpec((tm, ROPE_LANES), lambda i: (i % tab_blocks, 0))
    return pl.pallas_call(
        _mla_proj_kernel,
        grid=(m // tm,),
        in_specs=[row(d), _const_spec((1, d)), _const_spec((1, Q_LORA)), _const_spec((1, KV_LORA)),
                  _const_spec(wcat.shape), _const_spec(wuq.shape), tab, tab],
        out_specs=[row(n_nope), row(n_rope), row(KV_LORA), row(KV_LORA), row(QK_ROPE), row(ROPE_LANES)],
        out_shape=[jax.ShapeDtypeStruct((m, n_nope), BF16), jax.ShapeDtypeStruct((m, n_rope), BF16),
                   jax.ShapeDtypeStruct((m, KV_LORA), F32), jax.ShapeDtypeStruct((m, KV_LORA), BF16),
                   jax.ShapeDtypeStruct((m, QK_ROPE), F32), jax.ShapeDtypeStruct((m, ROPE_LANES), BF16)],
        compiler_params=_cparams("arbitrary"),
        name="mla_proj",
    )(h, g2.reshape(1, d), g_q.reshape(1, Q_LORA), g_kv.reshape(1, KV_LORA), wcat, wuq, cos_t, sin_t)


def _prompt_attn_kernel(qn_ref, qr_ref, latb_ref, kr2_ref, wkv_ref, o_ref, k_sc, v_sc, *, tq):
    head = pl.program_id(1)
    t = latb_ref.shape[0]
    kv = _dot(latb_ref[...], wkv_ref[0])
    k_sc[:, :QK_NOPE] = kv[:, :QK_NOPE].astype(BF16)
    lane = lax.broadcasted_iota(jnp.int32, (t, ROPE_LANES), 1)
    mine = (lane // QK_ROPE) == (head % 2)
    k_sc[:, QK_NOPE:] = jnp.where(mine, kr2_ref[...], jnp.zeros_like(kr2_ref[...]))
    v_sc[...] = kv[:, QK_NOPE:].astype(BF16)

    rows = lax.broadcasted_iota(jnp.int32, (tq, tq), 0)
    cols = lax.broadcasted_iota(jnp.int32, (tq, tq), 1)
    causal = cols <= rows
    for qi in range(t // tq):
        own = slice(qi * tq, (qi + 1) * tq)
        past = slice(0, qi * tq)
        q = jnp.concatenate([qn_ref[own, :], qr_ref[own, :]], axis=1)
        s_own = jnp.where(causal, _dot_nt(q, k_sc[own, :]) * ATTN_SCALE, NEG_INF)
        m = jnp.max(s_own, axis=1, keepdims=True)
        if qi:
            s_past = _dot_nt(q, k_sc[past, :]) * ATTN_SCALE
            m = jnp.maximum(m, jnp.max(s_past, axis=1, keepdims=True))
        p_own = jnp.exp(s_own - m)
        l = jnp.sum(p_own, axis=1, keepdims=True)
        acc = _dot(p_own.astype(BF16), v_sc[own, :])
        if qi:
            p_past = jnp.exp(s_past - m)
            l = l + jnp.sum(p_past, axis=1, keepdims=True)
            acc = acc + _dot(p_past.astype(BF16), v_sc[past, :])
        o_ref[own, :] = (acc / l).astype(BF16)


def _prompt_attn(qn, qr, latb, kr2, wkv_h, *, b, t, tq):
    return pl.pallas_call(
        functools.partial(_prompt_attn_kernel, tq=tq),
        grid=(b, N_HEADS),
        in_specs=[
            pl.BlockSpec((t, QK_NOPE), lambda bi, h: (bi, h)),
            pl.BlockSpec((t, ROPE_LANES), lambda bi, h: (bi, h // 2)),
            pl.BlockSpec((t, KV_LORA), lambda bi, h: (bi, 0)),
            pl.BlockSpec((t, ROPE_LANES), lambda bi, h: (bi, 0)),
            pl.BlockSpec((1, KV_LORA, QK_NOPE + V_DIM), lambda bi, h: (h, 0, 0)),
        ],
        out_specs=pl.BlockSpec((t, V_DIM), lambda bi, h: (bi, h)),
        out_shape=jax.ShapeDtypeStruct((b * t, N_HEADS * V_DIM), BF16),
        scratch_shapes=[pltpu.VMEM((t, QK_NOPE + ROPE_LANES), BF16), pltpu.VMEM((t, V_DIM), BF16)],
        compiler_params=_cparams("arbitrary", "arbitrary"),
        name="prompt_attn",
    )(qn, qr, latb, kr2, wkv_h)


def _bmm_kernel(x_ref, w_ref, o_ref):
    o_ref[...] = _dot(x_ref[...], w_ref[...]).astype(o_ref.dtype)


def _absorb_q(qn, wuk_t):
    m = qn.shape[0]
    return pl.pallas_call(
        _bmm_kernel,
        grid=(N_HEADS,),
        in_specs=[pl.BlockSpec((m, QK_NOPE), lambda h: (0, h)),
                  pl.BlockSpec((None, QK_NOPE, KV_LORA), lambda h: (h, 0, 0))],
        out_specs=pl.BlockSpec((None, m, KV_LORA), lambda h: (h, 0, 0)),
        out_shape=jax.ShapeDtypeStruct((N_HEADS, m, KV_LORA), BF16),
        compiler_params=_cparams("arbitrary"),
        name="absorb_q",
    )(qn, wuk_t)


def _expand_o(o_lat, wuv_h):
    m = o_lat.shape[1]
    return pl.pallas_call(
        _bmm_kernel,
        grid=(N_HEADS,),
        in_specs=[pl.BlockSpec((None, m, KV_LORA), lambda h: (h, 0, 0)),
                  pl.BlockSpec((None, KV_LORA, V_DIM), lambda h: (h, 0, 0))],
        out_specs=pl.BlockSpec((m, V_DIM), lambda h: (0, h)),
        out_shape=jax.ShapeDtypeStruct((m, N_HEADS * V_DIM), BF16),
        compiler_params=_cparams("arbitrary"),
        name="expand_o",
    )(o_lat, wuv_h)


PAGES_PER_STEP = 32


def _sample_attn_kernel(pt_ref, ql_ref, qr_ref, latn_ref, krn_ref, *rest, n_steps, t_new):
    del pt_ref
    lat_refs = rest[:PAGES_PER_STEP]
    kr_refs = rest[PAGES_PER_STEP:2 * PAGES_PER_STEP]
    o_ref, m_sc, l_sc, acc_sc = rest[2 * PAGES_PER_STEP:]
    j = pl.program_id(1)
    rows = ql_ref.shape[1]

    @pl.when(j == 0)
    def _():
        m_sc[...] = jnp.full(m_sc.shape, NEG_INF, F32)
        l_sc[...] = jnp.zeros(l_sc.shape, F32)
        acc_sc[...] = jnp.zeros(acc_sc.shape, F32)

    ql = ql_ref[0]
    qr = qr_ref[0]
    lats = [lat_refs[i][0, 0].astype(BF16) for i in range(PAGES_PER_STEP)]
    kr_t = jnp.concatenate([kr_refs[i][0, 0] for i in range(PAGES_PER_STEP)], axis=1).astype(BF16)
    s = (jnp.concatenate([_dot_nt(ql, lats[i]) for i in range(PAGES_PER_STEP)], axis=1)
         + _dot(qr, kr_t)) * ATTN_SCALE
    m_prev = m_sc[...]
    m_new = jnp.maximum(m_prev, jnp.max(s, axis=1, keepdims=True))
    alpha = jnp.exp(m_prev - m_new)
    p = jnp.exp(s - m_new)
    l_new = alpha * l_sc[...] + jnp.sum(p, axis=1, keepdims=True)
    pb = p.astype(BF16)
    pv = _dot(pb[:, :PAGE_SIZE], lats[0])
    for i in range(1, PAGES_PER_STEP):
        pv += _dot(pb[:, i * PAGE_SIZE:(i + 1) * PAGE_SIZE], lats[i])
    acc_new = alpha * acc_sc[...] + pv
    m_sc[...] = m_new
    l_sc[...] = l_new
    acc_sc[...] = acc_new

    @pl.when(j == n_steps - 1)
    def _():
        latn = latn_ref[0].astype(BF16).astype(F32)
        krn = krn_ref[0].astype(BF16).astype(F32)
        qlf = ql.astype(F32)
        qrf = qr.astype(F32)
        tok = lax.broadcasted_iota(jnp.int32, (rows, 1), 0) % t_new
        s_new = []
        for k in range(t_new):
            sk = (jnp.sum(qlf * latn[k:k + 1, :], axis=1, keepdims=True)
                  + jnp.sum(qrf * krn[k:k + 1, :], axis=1, keepdims=True)) * ATTN_SCALE
            s_new.append(jnp.where(k <= tok, sk, NEG_INF))
        m_fin = m_new
        for sk in s_new:
            m_fin = jnp.maximum(m_fin, sk)
        a_fin = jnp.exp(m_new - m_fin)
        l_fin = a_fin * l_new
        acc = a_fin * acc_new
        for k in range(t_new):
            pk = jnp.exp(s_new[k] - m_fin)
            l_fin = l_fin + pk
            acc = acc + pk.astype(BF16).astype(F32) * latn[k:k + 1, :]
        o_ref[0] = (acc / l_fin).astype(BF16)


def _sample_attn(ql, qr, lat_new, kr_new, cache_latent, cache_krope_t, page_table, layer):
    b, rows, _ = ql.shape
    t_new = lat_new.shape[1]
    n_steps = page_table.shape[1] // PAGES_PER_STEP

    def page(i):
        return lambda bi, j, pt: (layer, pt[bi, j * PAGES_PER_STEP + i], 0, 0)

    per_seq = lambda n, w: pl.BlockSpec((1, n, w), lambda bi, j, pt: (bi, 0, 0))
    grid_spec = pltpu.PrefetchScalarGridSpec(
        num_scalar_prefetch=1,
        grid=(b, n_steps),
        in_specs=[per_seq(rows, KV_LORA), per_seq(rows, QK_ROPE), per_seq(t_new, KV_LORA), per_seq(t_new, QK_ROPE)]
        + [pl.BlockSpec((1, 1, PAGE_SIZE, KV_LORA), page(i)) for i in range(PAGES_PER_STEP)]
        + [pl.BlockSpec((1, 1, QK_ROPE, PAGE_SIZE), page(i)) for i in range(PAGES_PER_STEP)],
        out_specs=per_seq(rows, KV_LORA),
        scratch_shapes=[pltpu.VMEM((rows, 1), F32), pltpu.VMEM((rows, 1), F32), pltpu.VMEM((rows, KV_LORA), F32)],
    )
    return pl.pallas_call(
        functools.partial(_sample_attn_kernel, n_steps=n_steps, t_new=t_new),
        grid_spec=grid_spec,
        out_shape=jax.ShapeDtypeStruct((b, rows, KV_LORA), BF16),
        compiler_params=_cparams("arbitrary", "arbitrary"),
        name="sample_attn",
    )(page_table, ql, qr, lat_new, kr_new, *([cache_latent] * PAGES_PER_STEP), *([cache_krope_t] * PAGES_PER_STEP))


def _out_proj_kernel(x_ref, w_ref, h_ref, g_ref, o_ref):
    y = _dot(x_ref[...], w_ref[...])
    o_ref[...] = h_ref[...] + _rms_rows(y, g_ref[...])


def _out_proj(x, w, h, g, *, tm):
    m, d = h.shape
    row = lambda n: pl.BlockSpec((tm, n), lambda i: (i, 0))
    return pl.pallas_call(
        _out_proj_kernel,
        grid=(m // tm,),
        in_specs=[row(x.shape[1]), _const_spec(w.shape), row(d), _const_spec((1, d))],
        out_specs=row(d),
        out_shape=jax.ShapeDtypeStruct((m, d), F32),
        compiler_params=_cparams("arbitrary"),
        name="out_proj",
    )(x, w, h, g.reshape(1, d))


HIST_ROWS = 16


def _pool_mix_cols(window_sum, u_cols, cnt, w_g, scale_cols):
    pooled = window_sum / cnt - u_cols
    return _dot(pooled.astype(BF16), w_g) * scale_cols


def _pool_prompt_kernel(h_ref, g2_ref, g3_ref, scale_ref, w_ref, o_ref, st_ref, ubuf, *, tm, n_t):
    ti = pl.program_id(1)

    @pl.when(ti == 0)
    def _():
        ubuf[0:HIST_ROWS, :] = jnp.zeros((HIST_ROWS, ubuf.shape[1]), F32)

    @pl.when(ti > 0)
    def _():
        ubuf[0:HIST_ROWS, :] = ubuf[tm:tm + HIST_ROWS, :]

    h = h_ref[...]
    ubuf[HIST_ROWS:HIST_ROWS + tm, :] = _rms_rows(h, g2_ref[...])
    pos = ti * tm + lax.broadcasted_iota(jnp.int32, (tm, 1), 0)
    for g, w in enumerate(POOL_WINDOWS):
        cols = slice(g * POOL_GROUP, (g + 1) * POOL_GROUP)
        u_cols = ubuf[HIST_ROWS:HIST_ROWS + tm, cols]
        win = u_cols
        for k in range(1, w):
            win = win + ubuf[HIST_ROWS - k:HIST_ROWS - k + tm, cols]
        cnt = jnp.minimum(pos + 1, w).astype(F32)
        o_ref[:, cols] = _pool_mix_cols(win, u_cols, cnt, w_ref[g], scale_ref[:, cols])
    o_ref[...] = h + _rms_rows(o_ref[...], g3_ref[...])

    @pl.when(ti == n_t - 1)
    def _():
        st_ref[0] = ubuf[HIST_ROWS + tm - POOL_HIST:HIST_ROWS + tm, :]


def _pool_prompt(h, g2, g3, scale, w_bf, *, b, t, tm):
    d = h.shape[1]
    n_t = t // tm
    return pl.pallas_call(
        functools.partial(_pool_prompt_kernel, tm=tm, n_t=n_t),
        grid=(b, n_t),
        in_specs=[pl.BlockSpec((tm, d), lambda bi, ti: (bi * n_t + ti, 0)),
                  _const_spec((1, d)), _const_spec((1, d)), _const_spec((1, d)), _const_spec(w_bf.shape)],
        out_specs=[pl.BlockSpec((tm, d), lambda bi, ti: (bi * n_t + ti, 0)),
                   pl.BlockSpec((1, POOL_HIST, d), lambda bi, ti: (bi, 0, 0))],
        out_shape=[jax.ShapeDtypeStruct(h.shape, F32), jax.ShapeDtypeStruct((b, POOL_HIST, d), F32)],
        scratch_shapes=[pltpu.VMEM((HIST_ROWS + tm, d), F32)],
        compiler_params=_cparams("arbitrary", "arbitrary"),
        name="pool_prompt",
    )(h, g2.reshape(1, d), g3.reshape(1, d), scale.reshape(1, d), w_bf)


def _pool_sample_kernel(hist_ref, h_ref, g2_ref, g3_ref, scale_ref, w_ref, o_ref, st_ref, mix_sc, *, start_pos):
    t_new = h_ref.shape[0]
    nb = h_ref.shape[1]
    hs = [h_ref[t] for t in range(t_new)]
    ext = [hist_ref[k] for k in range(POOL_HIST)] + [_rms_rows(x, g2_ref[...]) for x in hs]
    for t in range(t_new):
        for g, w in enumerate(POOL_WINDOWS):
            cols = slice(g * POOL_GROUP, (g + 1) * POOL_GROUP)
            u_cols = ext[POOL_HIST + t][:, cols]
            win = u_cols
            for k in range(1, w):
                win = win + ext[POOL_HIST + t - k][:, cols]
            cnt = float(min(start_pos + t + 1, w))
            mix_sc[t * nb:(t + 1) * nb, cols] = _pool_mix_cols(win, u_cols, cnt, w_ref[g], scale_ref[:, cols])
    for t in range(t_new):
        o_ref[t] = hs[t] + _rms_rows(mix_sc[t * nb:(t + 1) * nb, :], g3_ref[...])
    for k in range(POOL_HIST):
        st_ref[k] = ext[t_new + k]


def _pool_sample(hist_tm, h_tm, g2, g3, scale, w_bf, *, nb, start_pos):
    t_new, b, d = h_tm.shape
    seq = lambda n: pl.BlockSpec((n, nb, d), lambda i: (0, i, 0))
    return pl.pallas_call(
        functools.partial(_pool_sample_kernel, start_pos=start_pos),
        grid=(b // nb,),
        in_specs=[seq(POOL_HIST), seq(t_new), _const_spec((1, d)), _const_spec((1, d)), _const_spec((1, d)),
                  _const_spec(w_bf.shape)],
        out_specs=[seq(t_new), seq(POOL_HIST)],
        out_shape=[jax.ShapeDtypeStruct(h_tm.shape, F32), jax.ShapeDtypeStruct(hist_tm.shape, F32)],
        scratch_shapes=[pltpu.VMEM((t_new * nb, d), F32)],
        compiler_params=_cparams("arbitrary"),
        name="pool_sample",
    )(hist_tm, h_tm, g2.reshape(1, d), g3.reshape(1, d), scale.reshape(1, d), w_bf)


def _s5_params_kernel(lr_ref, li_ref, logdt_ref, br_ref, bi_ref, ar_ref, ai_ref, bbr_ref, bbi_ref):
    lr = lr_ref[...]
    li = li_ref[...]
    dt = jnp.exp(logdt_ref[...])
    mag = jnp.exp(lr * dt)
    a_re = mag * jnp.cos(li * dt)
    a_im = mag * jnp.sin(li * dt)
    den = lr * lr + li * li
    nr = a_re - 1.0
    ni = a_im
    zoh_re = (nr * lr + ni * li) / den
    zoh_im = (ni * lr - nr * li) / den
    ar_ref[...] = a_re
    ai_ref[...] = a_im
    for c in range(S5_GROUP):
        b_re = br_ref[c]
        b_im = bi_ref[c]
        bbr_ref[c] = zoh_re * b_re - zoh_im * b_im
        bbi_ref[c] = zoh_re * b_im + zoh_im * b_re


def _s5_params(lam_re, lam_im, log_dt, b_re_cgp, b_im_cgp):
    g, p = lam_re.shape
    full = lambda shape: pl.BlockSpec(shape, lambda: (0,) * len(shape))
    gp = jax.ShapeDtypeStruct((g, p), F32)
    cgp = jax.ShapeDtypeStruct(b_re_cgp.shape, F32)
    return pl.pallas_call(
        _s5_params_kernel,
        in_specs=[full((g, p)), full((g, p)), full((g, 1)), full(b_re_cgp.shape), full(b_im_cgp.shape)],
        out_specs=[full((g, p)), full((g, p)), full(b_re_cgp.shape), full(b_re_cgp.shape)],
        out_shape=[gp, gp, cgp, cgp],
        name="s5_params",
    )(lam_re, lam_im, log_dt.reshape(g, 1), b_re_cgp, b_im_cgp)


def _rms_kernel(h_ref, g_ref, o_ref):
    o_ref[...] = _rms_rows(h_ref[...], g_ref[...])


def _rms_call(h, g, *, tm):
    m, d = h.shape
    row = pl.BlockSpec((tm, d), lambda i: (i, 0))
    return pl.pallas_call(
        _rms_kernel, grid=(m // tm,), in_specs=[row, _const_spec((1, d))], out_specs=row,
        out_shape=jax.ShapeDtypeStruct((m, d), F32), compiler_params=_cparams("arbitrary"), name="rms_rows",
    )(h, g.reshape(1, d))


S5_BLOCK_GROUPS = 16
S5_BLOCK_IN = S5_BLOCK_GROUPS * S5_GROUP
S5_BLOCK_STATE = S5_BLOCK_GROUPS * S5_STATE
S5_BLOCKS = S5_GROUPS // S5_BLOCK_GROUPS
SCAN_UNROLL = 8


def _s5_scan_kernel(u_ref, a_ref, b_ref, c_ref, d_ref, h0r_ref, h0i_ref, y_ref, hr_ref, hi_ref,
                    buf, cr_sc, ci_sc, *, rows_per_step, steps, n_chunks):
    c = pl.program_id(1)
    rps = rows_per_step
    ns = S5_BLOCK_STATE

    @pl.when(c == 0)
    def _():
        cr_sc[...] = h0r_ref[...]
        ci_sc[...] = h0i_ref[...]

    u = u_ref[...]
    buf[...] = _dot(u.astype(BF16), b_ref[0])
    a = a_ref[0]
    a_re = jnp.broadcast_to(a[:, :ns], (rps, ns))
    a_im = jnp.broadcast_to(a[:, ns:], (rps, ns))

    def step(t, carry):
        h_re, h_im = carry
        r0 = pl.multiple_of(t * rps, rps)
        n_re = a_re * h_re - a_im * h_im + buf[pl.ds(r0, rps), :ns]
        n_im = a_re * h_im + a_im * h_re + buf[pl.ds(r0, rps), ns:]
        buf[pl.ds(r0, rps), :ns] = n_re
        buf[pl.ds(r0, rps), ns:] = n_im
        return n_re, n_im

    carry = (cr_sc[...], ci_sc[...])
    if steps % SCAN_UNROLL == 0:
        carry = lax.fori_loop(0, steps, step, carry, unroll=SCAN_UNROLL)
    else:
        for t in range(steps):
            carry = step(t, carry)
    cr_sc[...] = carry[0]
    ci_sc[...] = carry[1]
    y = _dot(buf[...].astype(BF16), c_ref[0]) + d_ref[...] * u
    y_ref[...] = jax.nn.gelu(y).astype(BF16)

    @pl.when(c == n_chunks - 1)
    def _():
        hr_ref[...] = carry[0]
        hi_ref[...] = carry[1]


def _s5_scan(u_tm, a_blk, b_blk, c_blk, d_skip, h0_re, h0_im, *, rows_per_step, steps):
    m, d = u_tm.shape
    rows = rows_per_step * steps
    n_chunks = m // rows
    ns = S5_BLOCK_STATE
    st = pl.BlockSpec((rows_per_step, ns), lambda k, c: (0, k))
    st_shape = jax.ShapeDtypeStruct((rows_per_step, S5_BLOCKS * ns), F32)
    return pl.pallas_call(
        functools.partial(_s5_scan_kernel, rows_per_step=rows_per_step, steps=steps, n_chunks=n_chunks),
        grid=(S5_BLOCKS, n_chunks),
        in_specs=[pl.BlockSpec((rows, S5_BLOCK_IN), lambda k, c: (c, k)),
                  pl.BlockSpec((1, 1, 2 * ns), lambda k, c: (k, 0, 0)),
                  pl.BlockSpec((1, S5_BLOCK_IN, 2 * ns), lambda k, c: (k, 0, 0)),
                  pl.BlockSpec((1, 2 * ns, S5_BLOCK_IN), lambda k, c: (k, 0, 0)),
                  pl.BlockSpec((1, S5_BLOCK_IN), lambda k, c: (0, k)),
                  st, st],
        out_specs=[pl.BlockSpec((rows, S5_BLOCK_IN), lambda k, c: (c, k)), st, st],
        out_shape=[jax.ShapeDtypeStruct((m, d), BF16), st_shape, st_shape],
        scratch_shapes=[pltpu.VMEM((rows, 2 * ns), F32), pltpu.VMEM((rows_per_step, ns), F32),
                        pltpu.VMEM((rows_per_step, ns), F32)],
        compiler_params=_cparams("arbitrary", "arbitrary"),
        name="s5_scan",
    )(u_tm, a_blk, b_blk, c_blk, d_skip.reshape(1, d), h0_re, h0_im)


def _glu_kernel(y_ref, w_ref, h_ref, g_ref, o_ref):
    d = o_ref.shape[1]
    y = y_ref[...]
    z1 = _dot(y, w_ref[:, :d])
    z2 = _dot(y, w_ref[:, d:])
    o_ref[...] = h_ref[...] + _rms_rows(z1 * jax.nn.sigmoid(z2), g_ref[...])


def _glu_out(y, w_bf, h, g, *, tm):
    m, d = h.shape
    row = lambda n: pl.BlockSpec((tm, n), lambda i: (i, 0))
    return pl.pallas_call(
        _glu_kernel,
        grid=(m // tm,),
        in_specs=[row(d), _const_spec(w_bf.shape), row(d), _const_spec((1, d))],
        out_specs=row(d),
        out_shape=jax.ShapeDtypeStruct((m, d), F32),
        compiler_params=_cparams("arbitrary"),
        name="glu_out",
    )(y, w_bf, h, g.reshape(1, d))


def _rope_tables(pos):
    half = QK_ROPE // 2
    inv = ROPE_THETA ** (-jnp.arange(half, dtype=F32) / half)
    ang = pos.astype(F32)[:, None] * inv[None, :]
    reps = ROPE_LANES // half
    return jnp.tile(jnp.cos(ang), (1, reps)), jnp.tile(jnp.sin(ang), (1, reps))


def _rot_cols(w):
    half = QK_ROPE // 2
    return jnp.concatenate([-w[..., half:], w[..., :half]], axis=-1)


def _mla_weights(w_dq, w_uq, w_dkv, w_uk, w_uv, w_o):
    r = w_uq.shape[0]
    wr = w_uq[..., QK_NOPE:]
    wuq = jnp.concatenate([w_uq[..., :QK_NOPE].reshape(r, -1), wr.reshape(r, -1), _rot_cols(wr).reshape(r, -1)],
                          axis=1).astype(BF16)
    wk = w_dkv[:, KV_LORA:]
    wkr = _rot_cols(wk)
    wcat = jnp.concatenate([w_dq, w_dkv[:, :KV_LORA], wk, wk, wkr, wkr], axis=1).astype(BF16)
    wuk_t = w_uk.transpose(1, 2, 0).astype(BF16)
    wuv_h = w_uv.transpose(1, 0, 2).astype(BF16)
    wkv_h = jnp.concatenate([w_uk.transpose(1, 0, 2).astype(BF16), wuv_h], axis=2)
    return wcat, wuq, wkv_h, wuk_t, wuv_h, w_o.astype(BF16)


def _s5_block_mats(a_re, a_im, bb_re, bb_im, c_re, c_im):
    nb, gb, ng, ns = S5_BLOCKS, S5_BLOCK_GROUPS, S5_GROUP, S5_STATE
    eye = jnp.eye(gb, dtype=F32)
    bb = jnp.stack([bb_re, bb_im])
    bb = bb.reshape(2, ng, nb, gb, ns).transpose(2, 3, 1, 0, 4)
    b_blk = (bb[:, :, :, :, None, :] * eye[None, :, None, None, :, None]).reshape(nb, gb * ng, 2 * gb * ns)
    cc = jnp.stack([c_re, -c_im])
    cc = cc.reshape(2, nb, gb, ng, ns).transpose(1, 0, 2, 4, 3)
    c_blk = (cc[:, :, :, :, None, :] * eye[None, None, :, None, :, None]).reshape(nb, 2 * gb * ns, gb * ng)
    a_blk = jnp.concatenate([a_re.reshape(nb, gb * ns), a_im.reshape(nb, gb * ns)], axis=1)
    return a_blk.reshape(nb, 1, 2 * gb * ns), b_blk.astype(BF16), c_blk.astype(BF16)


PROMPT_ROWS_PER_STEP = 8
PROMPT_SCAN_STEPS = 128


def kernel(x_prompt, x_sample, cache_latent, cache_krope, page_table, state_pool, state_s5_re, state_s5_im,
           norm_gains, ffn_w_gate, ffn_w_up, ffn_w_down,
           mla_w_dq, mla_g_q, mla_w_uq, mla_w_dkv, mla_g_kv, mla_w_uk, mla_w_uv, mla_w_o,
           pool_w, pool_scale,
           s5_lam_re, s5_lam_im, s5_log_dt, s5_b_re, s5_b_im, s5_c_re, s5_c_im, s5_d, s5_w_glu):
    bp, tp, d = x_prompt.shape
    bs, ts, _ = x_sample.shape
    hp = x_prompt.reshape(bp * tp, d)
    hs = x_sample.reshape(bs * ts, d)
    cos_p, sin_p = _rope_tables(jnp.arange(tp))
    cos_s, sin_s = _rope_tables(PAST_LEN + jnp.arange(ts))
    cos_s, sin_s = jnp.tile(cos_s, (bs, 1)), jnp.tile(sin_s, (bs, 1))
    cache_krope_t = cache_krope.transpose(0, 1, 3, 2)
    lat_p, kr_p, lat_s, kr_s = [], [], [], []
    pool_p, pool_s = [], []
    s5r_p, s5i_p, s5r_s, s5i_s = [], [], [], []
    tm_p, tm_s = 1024, bs * ts
    rows_tile = 512
    rows_tile_s = min(rows_tile, bs * ts)

    def ffn(h, g_pre, g_post, i, k, tm):
        return _half_ffn(h, g_pre, g_post, ffn_w_gate, ffn_w_up, ffn_w_down, layer=i, half=k, tm=tm, tf=256)

    for i in range(DEPTH):
        g = norm_gains[i]
        j = i // N_MIXERS
        kind = i % N_MIXERS
        hp = ffn(hp, g[0], g[1], i, 0, tm_p)
        hs = ffn(hs, g[0], g[1], i, 0, tm_s)
        if kind == 0:
            wcat, wuq, wkv_h, wuk_t, wuv_h, wo = _mla_weights(
                mla_w_dq[j], mla_w_uq[j], mla_w_dkv[j], mla_w_uk[j], mla_w_uv[j], mla_w_o[j])
            proj = functools.partial(_mla_proj, g2=g[2], g_q=mla_g_q[j], g_kv=mla_g_kv[j], wcat=wcat, wuq=wuq)
            qn, qr, lat, latb, kr, kr2 = proj(hp, cos_t=cos_p, sin_t=sin_p, tm=rows_tile)
            att = _prompt_attn(qn, qr, latb, kr2, wkv_h, b=bp, t=tp, tq=min(512, tp))
            hp = _out_proj(att, wo, hp, g[3], tm=rows_tile)
            lat_p.append(lat.reshape(bp, tp, KV_LORA))
            kr_p.append(kr.reshape(bp, tp, QK_ROPE))

            qn, qr, lat, _, kr, _ = proj(hs, cos_t=cos_s, sin_t=sin_s, tm=rows_tile_s)
            ql = _absorb_q(qn, wuk_t)
            ql = ql.reshape(N_HEADS, bs, ts, KV_LORA).transpose(1, 0, 2, 3).reshape(bs, N_HEADS * ts, KV_LORA)
            qrr = qr.reshape(bs, ts, N_HEADS, QK_ROPE).transpose(0, 2, 1, 3).reshape(bs, N_HEADS * ts, QK_ROPE)
            lat = lat.reshape(bs, ts, KV_LORA)
            kr = kr.reshape(bs, ts, QK_ROPE)
            o_lat = _sample_attn(ql, qrr, lat, kr, cache_latent, cache_krope_t, page_table, j)
            o_lat = o_lat.reshape(bs, N_HEADS, ts, KV_LORA).transpose(1, 0, 2, 3).reshape(N_HEADS, bs * ts, KV_LORA)
            hs = _out_proj(_expand_o(o_lat, wuv_h), wo, hs, g[3], tm=rows_tile_s)
            lat_s.append(lat)
            kr_s.append(kr)
        elif kind == 1:
            wp = pool_w[j].astype(BF16)
            hp, st = _pool_prompt(hp, g[2], g[3], pool_scale[j], wp, b=bp, t=tp, tm=rows_tile)
            pool_p.append(st)
            h_tm = hs.reshape(bs, ts, d).transpose(1, 0, 2)
            h_tm, st_tm = _pool_sample(state_pool[j].transpose(1, 0, 2), h_tm, g[2], g[3], pool_scale[j], wp,
                                       nb=min(32, bs), start_pos=PAST_LEN)
            hs = h_tm.transpose(1, 0, 2).reshape(bs * ts, d)
            pool_s.append(st_tm.transpose(1, 0, 2))
        else:
            a_re, a_im, bb_re, bb_im = _s5_params(s5_lam_re[j], s5_lam_im[j], s5_log_dt[j],
                                                  s5_b_re[j].transpose(2, 0, 1), s5_b_im[j].transpose(2, 0, 1))
            a_blk, b_blk, c_blk = _s5_block_mats(a_re, a_im, bb_re, bb_im, s5_c_re[j], s5_c_im[j])
            wglu = s5_w_glu[j].astype(BF16)
            n_state = S5_GROUPS * S5_STATE

            rp = PROMPT_ROWS_PER_STEP
            h_tm = hp.reshape(bp, tp, d).transpose(1, 0, 2)
            h_pad = jnp.pad(h_tm, ((0, 0), (0, rp - bp), (0, 0))).reshape(tp * rp, d)
            zero_state = jnp.zeros((rp, n_state), F32)
            y, sr, si = _s5_scan(_rms_call(h_pad, g[2], tm=1024), a_blk, b_blk, c_blk, s5_d[j], zero_state, zero_state,
                                 rows_per_step=rp, steps=PROMPT_SCAN_STEPS)
            y = y.reshape(tp, rp, d)[:, :bp].reshape(tp * bp, d)
            h_tm = _glu_out(y, wglu, h_tm.reshape(tp * bp, d), g[3], tm=rows_tile)
            hp = h_tm.reshape(tp, bp, d).transpose(1, 0, 2).reshape(bp * tp, d)
            s5r_p.append(sr[:bp].reshape(bp, S5_GROUPS, S5_STATE))
            s5i_p.append(si[:bp].reshape(bp, S5_GROUPS, S5_STATE))

            h_tm = hs.reshape(bs, ts, d).transpose(1, 0, 2).reshape(ts * bs, d)
            y, sr, si = _s5_scan(_rms_call(h_tm, g[2], tm=rows_tile_s), a_blk, b_blk, c_blk, s5_d[j],
                                 state_s5_re[j].reshape(bs, n_state), state_s5_im[j].reshape(bs, n_state),
                                 rows_per_step=bs, steps=ts)
            h_tm = _glu_out(y, wglu, h_tm, g[3], tm=rows_tile_s)
            hs = h_tm.reshape(ts, bs, d).transpose(1, 0, 2).reshape(bs * ts, d)
            s5r_s.append(sr.reshape(bs, S5_GROUPS, S5_STATE))
            s5i_s.append(si.reshape(bs, S5_GROUPS, S5_STATE))
        hp = ffn(hp, g[4], g[5], i, 1, tm_p)
        hs = ffn(hs, g[4], g[5], i, 1, tm_s)
    return (hp.reshape(bp, tp, d), hs.reshape(bs, ts, d),
            jnp.stack(lat_p), jnp.stack(kr_p), jnp.stack(lat_s), jnp.stack(kr_s),
            jnp.stack(pool_p), jnp.stack(pool_s),
            jnp.stack(s5r_p), jnp.stack(s5i_p), jnp.stack(s5r_s), jnp.stack(s5i_s))
```

```python
import functools
import math

import jax
import jax.numpy as jnp
from jax import lax
from jax.experimental import pallas as pl
from jax.experimental.pallas import tpu as pltpu

D_MODEL = 2048
DEPTH = 4
PAST_LEN = 8192
PAGE_SIZE = 128
N_MIXERS = 3
D_FF = 5632
MACARON_WEIGHT = 0.5
NORM_EPS = 1e-6
N_HEADS = 16
Q_LORA = 512
KV_LORA = 512
QK_NOPE = 128
QK_ROPE = 64
V_DIM = 128
ROPE_THETA = 10000.0
ATTN_SCALE = 1.0 / math.sqrt(QK_NOPE + QK_ROPE)
NEG_INF = -1e30
POOL_WINDOWS = (2, 4, 8, 16)
POOL_GROUP = D_MODEL // len(POOL_WINDOWS)
POOL_HIST = max(POOL_WINDOWS) - 1
S5_GROUP = 16
S5_GROUPS = D_MODEL // S5_GROUP
S5_STATE = 64

F32 = jnp.float32
BF16 = jnp.bfloat16

V7X_VMEM_BYTES = 64 * 1024 * 1024
VMEM_LIMIT_BYTES = V7X_VMEM_BYTES - 8 * 1024 * 1024

DOWN_COLS = 512
NORM_ROWS = 128


def _rms_rows(x, gain):
    ms = jnp.mean(x * x, axis=-1, keepdims=True)
    return x * lax.rsqrt(ms + NORM_EPS) * gain


def _cparams(*sem):
    return pltpu.CompilerParams(dimension_semantics=sem, vmem_limit_bytes=VMEM_LIMIT_BYTES)


def _const_spec(shape):
    nd = len(shape)
    return pl.BlockSpec(shape, lambda *_: (0,) * nd, pipeline_mode=pl.Buffered(1))


def _dot(a, b):
    return jnp.dot(a, b, preferred_element_type=F32)


def _dot_nt(a, b):
    return lax.dot_general(a, b, (((1,), (1,)), ((), ())), preferred_element_type=F32)


def _ffn_kernel(h_ref, gpre_ref, gpost_ref, wg_ref, wu_ref, wd_ref, o_ref, u_ref, *, n_f, tm):
    f = pl.program_id(1)
    rows = min(NORM_ROWS, tm)
    n_chunks = tm // rows

    @pl.when(f == 0)
    def _():
        def body(c, carry):
            r = pl.multiple_of(c * rows, rows)
            x = h_ref[pl.ds(r, rows), :]
            u_ref[pl.ds(r, rows), :] = _rms_rows(x, gpre_ref[...]).astype(BF16)
            o_ref[pl.ds(r, rows), :] = jnp.zeros((rows, o_ref.shape[1]), F32)
            return carry
        lax.fori_loop(0, n_chunks, body, 0)

    u = u_ref[...]
    gate = _dot(u, wg_ref[...].astype(BF16))
    up = _dot(u, wu_ref[...].astype(BF16))
    act = (gate * jax.nn.sigmoid(gate) * up).astype(BF16)
    wd = wd_ref[...].astype(BF16)
    d = o_ref.shape[1]
    for c in range(d // DOWN_COLS):
        cols = slice(c * DOWN_COLS, (c + 1) * DOWN_COLS)
        o_ref[:, cols] += _dot(act, wd[:, cols])

    @pl.when(f == n_f - 1)
    def _():
        def body(c, carry):
            r = pl.multiple_of(c * rows, rows)
            y = o_ref[pl.ds(r, rows), :]
            o_ref[pl.ds(r, rows), :] = (
                h_ref[pl.ds(r, rows), :] + MACARON_WEIGHT * _rms_rows(y, gpost_ref[...]))
            return carry
        lax.fori_loop(0, n_chunks, body, 0)


def _half_ffn(h, g_pre, g_post, w_gate, w_up, w_down, *, layer, half, tm, tf):
    m, d = h.shape
    n_f = D_FF // tf
    return pl.pallas_call(
        functools.partial(_ffn_kernel, n_f=n_f, tm=tm),
        grid=(m // tm, n_f),
        in_specs=[
            pl.BlockSpec((tm, d), lambda i, f: (i, 0)),
            pl.BlockSpec((1, d), lambda i, f: (0, 0)),
            pl.BlockSpec((1, d), lambda i, f: (0, 0)),
            pl.BlockSpec((None, None, d, tf), lambda i, f: (layer, half, 0, f)),
            pl.BlockSpec((None, None, d, tf), lambda i, f: (layer, half, 0, f)),
            pl.BlockSpec((None, None, tf, d), lambda i, f: (layer, half, f, 0)),
        ],
        out_specs=pl.BlockSpec((tm, d), lambda i, f: (i, 0)),
        out_shape=jax.ShapeDtypeStruct((m, d), F32),
        scratch_shapes=[pltpu.VMEM((tm, d), BF16)],
        compiler_params=_cparams("arbitrary", "arbitrary"),
        name="half_ffn",
    )(h, g_pre.reshape(1, d), g_post.reshape(1, d), w_gate, w_up, w_down)


ROPE_LANES = 128


def _mla_proj_kernel(h_ref, g2_ref, gq_ref, gkv_ref, wcat_ref, wuq_ref, cos_ref, sin_ref,
                     qn_ref, qr_ref, lat_ref, latb_ref, kr_ref, kr2_ref):
    u = _rms_rows(h_ref[...], g2_ref[...]).astype(BF16)
    r = _dot(u, wcat_ref[...])
    c = _rms_rows(r[:, :Q_LORA], gq_ref[...]).astype(BF16)
    lat = _rms_rows(r[:, Q_LORA:Q_LORA + KV_LORA], gkv_ref[...])
    lat_ref[...] = lat
    latb_ref[...] = lat.astype(BF16)
    cos = cos_ref[...]
    sin = sin_ref[...]
    k0 = Q_LORA + KV_LORA
    kr2 = r[:, k0:k0 + ROPE_LANES] * cos + r[:, k0 + ROPE_LANES:k0 + 2 * ROPE_LANES] * sin
    kr_ref[...] = kr2[:, :QK_ROPE]
    kr2_ref[...] = kr2.astype(BF16)
    q = _dot(c, wuq_ref[...])
    n_nope = N_HEADS * QK_NOPE
    n_rope = N_HEADS * QK_ROPE
    qn_ref[...] = q[:, :n_nope].astype(BF16)
    reps = n_rope // ROPE_LANES
    cos_q = jnp.concatenate([cos] * reps, axis=1)
    sin_q = jnp.concatenate([sin] * reps, axis=1)
    qr = q[:, n_nope:n_nope + n_rope] * cos_q + q[:, n_nope + n_rope:] * sin_q
    qr_ref[...] = qr.astype(BF16)


def _mla_proj(h, g2, g_q, g_kv, wcat, wuq, cos_t, sin_t, *, tm):
    m, d = h.shape
    tab_blocks = cos_t.shape[0] // tm
    n_nope = N_HEADS * QK_NOPE
    n_rope = N_HEADS * QK_ROPE
    row = lambda n: pl.BlockSpec((tm, n), lambda i: (i, 0))
    tab = pl.BlockSpec((tm, ROPE_LANES), lambda i: (i % tab_blocks, 0))
    return pl.pallas_call(
        _mla_proj_kernel,
        grid=(m // tm,),
        in_specs=[row(d), _const_spec((1, d)), _const_spec((1, Q_LORA)), _const_spec((1, KV_LORA)),
                  _const_spec(wcat.shape), _const_spec(wuq.shape), tab, tab],
        out_specs=[row(n_nope), row(n_rope), row(KV_LORA), row(KV_LORA), row(QK_ROPE), row(ROPE_LANES)],
        out_shape=[jax.ShapeDtypeStruct((m, n_nope), BF16), jax.ShapeDtypeStruct((m, n_rope), BF16),
                   jax.ShapeDtypeStruct((m, KV_LORA), F32), jax.ShapeDtypeStruct((m, KV_LORA), BF16),
                   jax.ShapeDtypeStruct((m, QK_ROPE), F32), jax.ShapeDtypeStruct((m, ROPE_LANES), BF16)],
        compiler_params=_cparams("arbitrary"),
        name="mla_proj",
    )(h, g2.reshape(1, d), g_q.reshape(1, Q_LORA), g_kv.reshape(1, KV_LORA), wcat, wuq, cos_t, sin_t)


def _prompt_attn_kernel(qn_ref, qr_ref, latb_ref, kr2_ref, wkv_ref, o_ref, k_sc, v_sc, *, tq):
    head = pl.program_id(1)
    t = latb_ref.shape[0]
    kv = _dot(latb_ref[...], wkv_ref[0])
    k_sc[:, :QK_NOPE] = kv[:, :QK_NOPE].astype(BF16)
    lane = lax.broadcasted_iota(jnp.int32, (t, ROPE_LANES), 1)
    mine = (lane // QK_ROPE) == (head % 2)
    k_sc[:, QK_NOPE:] = jnp.where(mine, kr2_ref[...], jnp.zeros_like(kr2_ref[...]))
    v_sc[...] = kv[:, QK_NOPE:].astype(BF16)

    rows = lax.broadcasted_iota(jnp.int32, (tq, tq), 0)
    cols = lax.broadcasted_iota(jnp.int32, (tq, tq), 1)
    causal = cols <= rows
    for qi in range(t // tq):
        own = slice(qi * tq, (qi + 1) * tq)
        past = slice(0, qi * tq)
        q = jnp.concatenate([qn_ref[own, :], qr_ref[own, :]], axis=1)
        s_own = jnp.where(causal, _dot_nt(q, k_sc[own, :]) * ATTN_SCALE, NEG_INF)
        m = jnp.max(s_own, axis=1, keepdims=True)
        if qi:
            s_past = _dot_nt(q, k_sc[past, :]) * ATTN_SCALE
            m = jnp.maximum(m, jnp.max(s_past, axis=1, keepdims=True))
        p_own = jnp.exp(s_own - m)
        l = jnp.sum(p_own, axis=1, keepdims=True)
        acc = _dot(p_own.astype(BF16), v_sc[own, :])
        if qi:
            p_past = jnp.exp(s_past - m)
            l = l + jnp.sum(p_past, axis=1, keepdims=True)
            acc = acc + _dot(p_past.astype(BF16), v_sc[past, :])
        o_ref[own, :] = (acc / l).astype(BF16)


def _prompt_attn(qn, qr, latb, kr2, wkv_h, *, b, t, tq):
    return pl.pallas_call(
        functools.partial(_prompt_attn_kernel, tq=tq),
        grid=(b, N_HEADS),
        in_specs=[
            pl.BlockSpec((t, QK_NOPE), lambda bi, h: (bi, h)),
            pl.BlockSpec((t, ROPE_LANES), lambda bi, h: (bi, h // 2)),
            pl.BlockSpec((t, KV_LORA), lambda bi, h: (bi, 0)),
            pl.BlockSpec((t, ROPE_LANES), lambda bi, h: (bi, 0)),
            pl.BlockSpec((1, KV_LORA, QK_NOPE + V_DIM), lambda bi, h: (h, 0, 0)),
        ],
        out_specs=pl.BlockSpec((t, V_DIM), lambda bi, h: (bi, h)),
        out_shape=jax.ShapeDtypeStruct((b * t, N_HEADS * V_DIM), BF16),
        scratch_shapes=[pltpu.VMEM((t, QK_NOPE + ROPE_LANES), BF16), pltpu.VMEM((t, V_DIM), BF16)],
        compiler_params=_cparams("arbitrary", "arbitrary"),
        name="prompt_attn",
    )(qn, qr, latb, kr2, wkv_h)


def _bmm_kernel(x_ref, w_ref, o_ref):
    o_ref[...] = _dot(x_ref[...], w_ref[...]).astype(o_ref.dtype)


def _absorb_q(qn, wuk_t):
    m = qn.shape[0]
    return pl.pallas_call(
        _bmm_kernel,
        grid=(N_HEADS,),
        in_specs=[pl.BlockSpec((m, QK_NOPE), lambda h: (0, h)),
                  pl.BlockSpec((None, QK_NOPE, KV_LORA), lambda h: (h, 0, 0))],
        out_specs=pl.BlockSpec((None, m, KV_LORA), lambda h: (h, 0, 0)),
        out_shape=jax.ShapeDtypeStruct((N_HEADS, m, KV_LORA), BF16),
        compiler_params=_cparams("arbitrary"),
        name="absorb_q",
    )(qn, wuk_t)


def _expand_o(o_lat, wuv_h):
    m = o_lat.shape[1]
    return pl.pallas_call(
        _bmm_kernel,
        grid=(N_HEADS,),
        in_specs=[pl.BlockSpec((None, m, KV_LORA), lambda h: (h, 0, 0)),
                  pl.BlockSpec((None, KV_LORA, V_DIM), lambda h: (h, 0, 0))],
        out_specs=pl.BlockSpec((m, V_DIM), lambda h: (0, h)),
        out_shape=jax.ShapeDtypeStruct((m, N_HEADS * V_DIM), BF16),
        compiler_params=_cparams("arbitrary"),
        name="expand_o",
    )(o_lat, wuv_h)


PAGES_PER_STEP = 64


def _sample_attn_kernel(pt_ref, ql_ref, qr_ref, latn_ref, krn_ref, *rest, n_steps, t_new):
    del pt_ref
    lat_refs = rest[:PAGES_PER_STEP]
    kr_refs = rest[PAGES_PER_STEP:2 * PAGES_PER_STEP]
    o_ref, m_sc, l_sc, acc_sc = rest[2 * PAGES_PER_STEP:]
    j = pl.program_id(1)
    rows = ql_ref.shape[1]

    @pl.when(j == 0)
    def _():
        m_sc[...] = jnp.full(m_sc.shape, NEG_INF, F32)
        l_sc[...] = jnp.zeros(l_sc.shape, F32)
        acc_sc[...] = jnp.zeros(acc_sc.shape, F32)

    ql = ql_ref[0]
    qr = qr_ref[0]
    lats = [lat_refs[i][0, 0].astype(BF16) for i in range(PAGES_PER_STEP)]
    kr_t = jnp.concatenate([kr_refs[i][0, 0] for i in range(PAGES_PER_STEP)], axis=1).astype(BF16)
    s = (jnp.concatenate([_dot_nt(ql, lats[i]) for i in range(PAGES_PER_STEP)], axis=1)
         + _dot(qr, kr_t)) * ATTN_SCALE
    m_prev = m_sc[...]
    m_new = jnp.maximum(m_prev, jnp.max(s, axis=1, keepdims=True))
    alpha = jnp.exp(m_prev - m_new)
    p = jnp.exp(s - m_new)
    l_new = alpha * l_sc[...] + jnp.sum(p, axis=1, keepdims=True)
    pb = p.astype(BF16)
    pv = _dot(pb[:, :PAGE_SIZE], lats[0])
    for i in range(1, PAGES_PER_STEP):
        pv += _dot(pb[:, i * PAGE_SIZE:(i + 1) * PAGE_SIZE], lats[i])
    acc_new = alpha * acc_sc[...] + pv
    m_sc[...] = m_new
    l_sc[...] = l_new
    acc_sc[...] = acc_new

    @pl.when(j == n_steps - 1)
    def _():
        latn = latn_ref[0].astype(BF16).astype(F32)
        krn = krn_ref[0].astype(BF16).astype(F32)
        qlf = ql.astype(F32)
        qrf = qr.astype(F32)
        tok = lax.broadcasted_iota(jnp.int32, (rows, 1), 0) % t_new
        s_new = []
        for k in range(t_new):
            sk = (jnp.sum(qlf * latn[k:k + 1, :], axis=1, keepdims=True)
                  + jnp.sum(qrf * krn[k:k + 1, :], axis=1, keepdims=True)) * ATTN_SCALE
            s_new.append(jnp.where(k <= tok, sk, NEG_INF))
        m_fin = m_new
        for sk in s_new:
            m_fin = jnp.maximum(m_fin, sk)
        a_fin = jnp.exp(m_new - m_fin)
        l_fin = a_fin * l_new
        acc = a_fin * acc_new
        for k in range(t_new):
            pk = jnp.exp(s_new[k] - m_fin)
            l_fin = l_fin + pk
            acc = acc + pk.astype(BF16).astype(F32) * latn[k:k + 1, :]
        o_ref[0] = (acc / l_fin).astype(BF16)


def _sample_attn(ql, qr, lat_new, kr_new, cache_latent, cache_krope_t, page_table, layer):
    b, rows, _ = ql.shape
    t_new = lat_new.shape[1]
    n_steps = page_table.shape[1] // PAGES_PER_STEP

    def page(i):
        return lambda bi, j, pt: (layer, pt[bi, j * PAGES_PER_STEP + i], 0, 0)

    per_seq = lambda n, w: pl.BlockSpec((1, n, w), lambda bi, j, pt: (bi, 0, 0))
    grid_spec = pltpu.PrefetchScalarGridSpec(
        num_scalar_prefetch=1,
        grid=(b, n_steps),
        in_specs=[per_seq(rows, KV_LORA), per_seq(rows, QK_ROPE), per_seq(t_new, KV_LORA), per_seq(t_new, QK_ROPE)]
        + [pl.BlockSpec((1, 1, PAGE_SIZE, KV_LORA), page(i)) for i in range(PAGES_PER_STEP)]
        + [pl.BlockSpec((1, 1, QK_ROPE, PAGE_SIZE), page(i)) for i in range(PAGES_PER_STEP)],
        out_specs=per_seq(rows, KV_LORA),
        scratch_shapes=[pltpu.VMEM((rows, 1), F32), pltpu.VMEM((rows, 1), F32), pltpu.VMEM((rows, KV_LORA), F32)],
    )
    return pl.pallas_call(
        functools.partial(_sample_attn_kernel, n_steps=n_steps, t_new=t_new),
        grid_spec=grid_spec,
        out_shape=jax.ShapeDtypeStruct((b, rows, KV_LORA), BF16),
        compiler_params=_cparams("arbitrary", "arbitrary"),
        name="sample_attn",
    )(page_table, ql, qr, lat_new, kr_new, *([cache_latent] * PAGES_PER_STEP), *([cache_krope_t] * PAGES_PER_STEP))


def _out_proj_kernel(x_ref, w_ref, h_ref, g_ref, o_ref):
    y = _dot(x_ref[...], w_ref[...])
    o_ref[...] = h_ref[...] + _rms_rows(y, g_ref[...])


def _out_proj(x, w, h, g, *, tm):
    m, d = h.shape
    row = lambda n: pl.BlockSpec((tm, n), lambda i: (i, 0))
    return pl.pallas_call(
        _out_proj_kernel,
        grid=(m // tm,),
        in_specs=[row(x.shape[1]), _const_spec(w.shape), row(d), _const_spec((1, d))],
        out_specs=row(d),
        out_shape=jax.ShapeDtypeStruct((m, d), F32),
        compiler_params=_cparams("arbitrary"),
        name="out_proj",
    )(x, w, h, g.reshape(1, d))


HIST_ROWS = 16


def _pool_mix_cols(window_sum, u_cols, cnt, w_g, scale_cols):
    pooled = window_sum / cnt - u_cols
    return _dot(pooled.astype(BF16), w_g) * scale_cols


def _pool_prompt_kernel(h_ref, g2_ref, g3_ref, scale_ref, w_ref, o_ref, st_ref, ubuf, *, tm, n_t):
    ti = pl.program_id(1)

    @pl.when(ti == 0)
    def _():
        ubuf[0:HIST_ROWS, :] = jnp.zeros((HIST_ROWS, ubuf.shape[1]), F32)

    @pl.when(ti > 0)
    def _():
        ubuf[0:HIST_ROWS, :] = ubuf[tm:tm + HIST_ROWS, :]

    h = h_ref[...]
    ubuf[HIST_ROWS:HIST_ROWS + tm, :] = _rms_rows(h, g2_ref[...])
    pos = ti * tm + lax.broadcasted_iota(jnp.int32, (tm, 1), 0)
    for g, w in enumerate(POOL_WINDOWS):
        cols = slice(g * POOL_GROUP, (g + 1) * POOL_GROUP)
        u_cols = ubuf[HIST_ROWS:HIST_ROWS + tm, cols]
        win = u_cols
        for k in range(1, w):
            win = win + ubuf[HIST_ROWS - k:HIST_ROWS - k + tm, cols]
        cnt = jnp.minimum(pos + 1, w).astype(F32)
        o_ref[:, cols] = _pool_mix_cols(win, u_cols, cnt, w_ref[g], scale_ref[:, cols])
    o_ref[...] = h + _rms_rows(o_ref[...], g3_ref[...])

    @pl.when(ti == n_t - 1)
    def _():
        st_ref[0] = ubuf[HIST_ROWS + tm - POOL_HIST:HIST_ROWS + tm, :]


def _pool_prompt(h, g2, g3, scale, w_bf, *, b, t, tm):
    d = h.shape[1]
    n_t = t // tm
    return pl.pallas_call(
        functools.partial(_pool_prompt_kernel, tm=tm, n_t=n_t),
        grid=(b, n_t),
        in_specs=[pl.BlockSpec((tm, d), lambda bi, ti: (bi * n_t + ti, 0)),
                  _const_spec((1, d)), _const_spec((1, d)), _const_spec((1, d)), _const_spec(w_bf.shape)],
        out_specs=[pl.BlockSpec((tm, d), lambda bi, ti: (bi * n_t + ti, 0)),
                   pl.BlockSpec((1, POOL_HIST, d), lambda bi, ti: (bi, 0, 0))],
        out_shape=[jax.ShapeDtypeStruct(h.shape, F32), jax.ShapeDtypeStruct((b, POOL_HIST, d), F32)],
        scratch_shapes=[pltpu.VMEM((HIST_ROWS + tm, d), F32)],
        compiler_params=_cparams("arbitrary", "arbitrary"),
        name="pool_prompt",
    )(h, g2.reshape(1, d), g3.reshape(1, d), scale.reshape(1, d), w_bf)


def _pool_sample_kernel(hist_ref, h_ref, g2_ref, g3_ref, scale_ref, w_ref, o_ref, st_ref, mix_sc, *, start_pos):
    t_new = h_ref.shape[0]
    nb = h_ref.shape[1]
    hs = [h_ref[t] for t in range(t_new)]
    ext = [hist_ref[k] for k in range(POOL_HIST)] + [_rms_rows(x, g2_ref[...]) for x in hs]
    for t in range(t_new):
        for g, w in enumerate(POOL_WINDOWS):
            cols = slice(g * POOL_GROUP, (g + 1) * POOL_GROUP)
            u_cols = ext[POOL_HIST + t][:, cols]
            win = u_cols
            for k in range(1, w):
                win = win + ext[POOL_HIST + t - k][:, cols]
            cnt = float(min(start_pos + t + 1, w))
            mix_sc[t * nb:(t + 1) * nb, cols] = _pool_mix_cols(win, u_cols, cnt, w_ref[g], scale_ref[:, cols])
    for t in range(t_new):
        o_ref[t] = hs[t] + _rms_rows(mix_sc[t * nb:(t + 1) * nb, :], g3_ref[...])
    for k in range(POOL_HIST):
        st_ref[k] = ext[t_new + k]


def _pool_sample(hist_tm, h_tm, g2, g3, scale, w_bf, *, nb, start_pos):
    t_new, b, d = h_tm.shape
    seq = lambda n: pl.BlockSpec((n, nb, d), lambda i: (0, i, 0))
    return pl.pallas_call(
        functools.partial(_pool_sample_kernel, start_pos=start_pos),
        grid=(b // nb,),
        in_specs=[seq(POOL_HIST), seq(t_new), _const_spec((1, d)), _const_spec((1, d)), _const_spec((1, d)),
                  _const_spec(w_bf.shape)],
        out_specs=[seq(t_new), seq(POOL_HIST)],
        out_shape=[jax.ShapeDtypeStruct(h_tm.shape, F32), jax.ShapeDtypeStruct(hist_tm.shape, F32)],
        scratch_shapes=[pltpu.VMEM((t_new * nb, d), F32)],
        compiler_params=_cparams("arbitrary"),
        name="pool_sample",
    )(hist_tm, h_tm, g2.reshape(1, d), g3.reshape(1, d), scale.reshape(1, d), w_bf)


def _s5_params_kernel(lr_ref, li_ref, logdt_ref, br_ref, bi_ref, ar_ref, ai_ref, bbr_ref, bbi_ref):
    lr = lr_ref[...]
    li = li_ref[...]
    dt = jnp.exp(logdt_ref[...])
    mag = jnp.exp(lr * dt)
    a_re = mag * jnp.cos(li * dt)
    a_im = mag * jnp.sin(li * dt)
    den = lr * lr + li * li
    nr = a_re - 1.0
    ni = a_im
    zoh_re = (nr * lr + ni * li) / den
    zoh_im = (ni * lr - nr * li) / den
    ar_ref[...] = a_re
    ai_ref[...] = a_im
    for c in range(S5_GROUP):
        b_re = br_ref[c]
        b_im = bi_ref[c]
        bbr_ref[c] = zoh_re * b_re - zoh_im * b_im
        bbi_ref[c] = zoh_re * b_im + zoh_im * b_re


def _s5_params(lam_re, lam_im, log_dt, b_re_cgp, b_im_cgp):
    g, p = lam_re.shape
    full = lambda shape: pl.BlockSpec(shape, lambda: (0,) * len(shape))
    gp = jax.ShapeDtypeStruct((g, p), F32)
    cgp = jax.ShapeDtypeStruct(b_re_cgp.shape, F32)
    return pl.pallas_call(
        _s5_params_kernel,
        in_specs=[full((g, p)), full((g, p)), full((g, 1)), full(b_re_cgp.shape), full(b_im_cgp.shape)],
        out_specs=[full((g, p)), full((g, p)), full(b_re_cgp.shape), full(b_re_cgp.shape)],
        out_shape=[gp, gp, cgp, cgp],
        name="s5_params",
    )(lam_re, lam_im, log_dt.reshape(g, 1), b_re_cgp, b_im_cgp)


def _rms_kernel(h_ref, g_ref, o_ref):
    o_ref[...] = _rms_rows(h_ref[...], g_ref[...])


def _rms_call(h, g, *, tm):
    m, d = h.shape
    row = pl.BlockSpec((tm, d), lambda i: (i, 0))
    return pl.pallas_call(
        _rms_kernel, grid=(m // tm,), in_specs=[row, _const_spec((1, d))], out_specs=row,
        out_shape=jax.ShapeDtypeStruct((m, d), F32), compiler_params=_cparams("arbitrary"), name="rms_rows",
    )(h, g.reshape(1, d))


S5_BLOCK_GROUPS = 16
S5_BLOCK_IN = S5_BLOCK_GROUPS * S5_GROUP
S5_BLOCK_STATE = S5_BLOCK_GROUPS * S5_STATE
S5_BLOCKS = S5_GROUPS // S5_BLOCK_GROUPS
SCAN_UNROLL = 8


def _s5_scan_kernel(u_ref, a_ref, b_ref, c_ref, d_ref, h0r_ref, h0i_ref, y_ref, hr_ref, hi_ref,
                    buf, cr_sc, ci_sc, *, rows_per_step, steps, n_chunks):
    c = pl.program_id(1)
    rps = rows_per_step
    ns = S5_BLOCK_STATE

    @pl.when(c == 0)
    def _():
        cr_sc[...] = h0r_ref[...]
        ci_sc[...] = h0i_ref[...]

    u = u_ref[...]
    buf[...] = _dot(u.astype(BF16), b_ref[0])
    a = a_ref[0]
    a_re = jnp.broadcast_to(a[:, :ns], (rps, ns))
    a_im = jnp.broadcast_to(a[:, ns:], (rps, ns))

    def step(t, carry):
        h_re, h_im = carry
        r0 = pl.multiple_of(t * rps, rps)
        n_re = a_re * h_re - a_im * h_im + buf[pl.ds(r0, rps), :ns]
        n_im = a_re * h_im + a_im * h_re + buf[pl.ds(r0, rps), ns:]
        buf[pl.ds(r0, rps), :ns] = n_re
        buf[pl.ds(r0, rps), ns:] = n_im
        return n_re, n_im

    carry = (cr_sc[...], ci_sc[...])
    if steps % SCAN_UNROLL == 0:
        carry = lax.fori_loop(0, steps, step, carry, unroll=SCAN_UNROLL)
    else:
        for t in range(steps):
            carry = step(t, carry)
    cr_sc[...] = carry[0]
    ci_sc[...] = carry[1]
    y = _dot(buf[...].astype(BF16), c_ref[0]) + d_ref[...] * u
    y_ref[...] = jax.nn.gelu(y).astype(BF16)

    @pl.when(c == n_chunks - 1)
    def _():
        hr_ref[...] = carry[0]
        hi_ref[...] = carry[1]


def _s5_scan(u_tm, a_blk, b_blk, c_blk, d_skip, h0_re, h0_im, *, rows_per_step, steps):
    m, d = u_tm.shape
    rows = rows_per_step * steps
    n_chunks = m // rows
    ns = S5_BLOCK_STATE
    st = pl.BlockSpec((rows_per_step, ns), lambda k, c: (0, k))
    st_shape = jax.ShapeDtypeStruct((rows_per_step, S5_BLOCKS * ns), F32)
    return pl.pallas_call(
        functools.partial(_s5_scan_kernel, rows_per_step=rows_per_step, steps=steps, n_chunks=n_chunks),
        grid=(S5_BLOCKS, n_chunks),
        in_specs=[pl.BlockSpec((rows, S5_BLOCK_IN), lambda k, c: (c, k)),
                  pl.BlockSpec((1, 1, 2 * ns), lambda k, c: (k, 0, 0)),
                  pl.BlockSpec((1, S5_BLOCK_IN, 2 * ns), lambda k, c: (k, 0, 0)),
                  pl.BlockSpec((1, 2 * ns, S5_BLOCK_IN), lambda k, c: (k, 0, 0)),
                  pl.BlockSpec((1, S5_BLOCK_IN), lambda k, c: (0, k)),
                  st, st],
        out_specs=[pl.BlockSpec((rows, S5_BLOCK_IN), lambda k, c: (c, k)), st, st],
        out_shape=[jax.ShapeDtypeStruct((m, d), BF16), st_shape, st_shape],
        scratch_shapes=[pltpu.VMEM((rows, 2 * ns), F32), pltpu.VMEM((rows_per_step, ns), F32),
                        pltpu.VMEM((rows_per_step, ns), F32)],
        compiler_params=_cparams("arbitrary", "arbitrary"),
        name="s5_scan",
    )(u_tm, a_blk, b_blk, c_blk, d_skip.reshape(1, d), h0_re, h0_im)


def _glu_kernel(y_ref, w_ref, h_ref, g_ref, o_ref):
    d = o_ref.shape[1]
    y = y_ref[...]
    z1 = _dot(y, w_ref[:, :d])
    z2 = _dot(y, w_ref[:, d:])
    o_ref[...] = h_ref[...] + _rms_rows(z1 * jax.nn.sigmoid(z2), g_ref[...])


def _glu_out(y, w_bf, h, g, *, tm):
    m, d = h.shape
    row = lambda n: pl.BlockSpec((tm, n), lambda i: (i, 0))
    return pl.pallas_call(
        _glu_kernel,
        grid=(m // tm,),
        in_specs=[row(d), _const_spec(w_bf.shape), row(d), _const_spec((1, d))],
        out_specs=row(d),
        out_shape=jax.ShapeDtypeStruct((m, d), F32),
        compiler_params=_cparams("arbitrary"),
        name="glu_out",
    )(y, w_bf, h, g.reshape(1, d))


def _rope_tables(pos):
    half = QK_ROPE // 2
    inv = ROPE_THETA ** (-jnp.arange(half, dtype=F32) / half)
    ang = pos.astype(F32)[:, None] * inv[None, :]
    reps = ROPE_LANES // half
    return jnp.tile(jnp.cos(ang), (1, reps)), jnp.tile(jnp.sin(ang), (1, reps))


def _rot_cols(w):
    half = QK_ROPE // 2
    return jnp.concatenate([-w[..., half:], w[..., :half]], axis=-1)


def _mla_weights(w_dq, w_uq, w_dkv, w_uk, w_uv, w_o):
    r = w_uq.shape[0]
    wr = w_uq[..., QK_NOPE:]
    wuq = jnp.concatenate([w_uq[..., :QK_NOPE].reshape(r, -1), wr.reshape(r, -1), _rot_cols(wr).reshape(r, -1)],
                          axis=1).astype(BF16)
    wk = w_dkv[:, KV_LORA:]
    wkr = _rot_cols(wk)
    wcat = jnp.concatenate([w_dq, w_dkv[:, :KV_LORA], wk, wk, wkr, wkr], axis=1).astype(BF16)
    wuk_t = w_uk.transpose(1, 2, 0).astype(BF16)
    wuv_h = w_uv.transpose(1, 0, 2).astype(BF16)
    wkv_h = jnp.concatenate([w_uk.transpose(1, 0, 2).astype(BF16), wuv_h], axis=2)
    return wcat, wuq, wkv_h, wuk_t, wuv_h, w_o.astype(BF16)


def _s5_block_mats(a_re, a_im, bb_re, bb_im, c_re, c_im):
    nb, gb, ng, ns = S5_BLOCKS, S5_BLOCK_GROUPS, S5_GROUP, S5_STATE
    eye = jnp.eye(gb, dtype=F32)
    bb = jnp.stack([bb_re, bb_im])
    bb = bb.reshape(2, ng, nb, gb, ns).transpose(2, 3, 1, 0, 4)
    b_blk = (bb[:, :, :, :, None, :] * eye[None, :, None, None, :, None]).reshape(nb, gb * ng, 2 * gb * ns)
    cc = jnp.stack([c_re, -c_im])
    cc = cc.reshape(2, nb, gb, ng, ns).transpose(1, 0, 2, 4, 3)
    c_blk = (cc[:, :, :, :, None, :] * eye[None, None, :, None, :, None]).reshape(nb, 2 * gb * ns, gb * ng)
    a_blk = jnp.concatenate([a_re.reshape(nb, gb * ns), a_im.reshape(nb, gb * ns)], axis=1)
    return a_blk.reshape(nb, 1, 2 * gb * ns), b_blk.astype(BF16), c_blk.astype(BF16)


PROMPT_ROWS_PER_STEP = 8
PROMPT_SCAN_STEPS = 128


def kernel(x_prompt, x_sample, cache_latent, cache_krope, page_table, state_pool, state_s5_re, state_s5_im,
           norm_gains, ffn_w_gate, ffn_w_up, ffn_w_down,
           mla_w_dq, mla_g_q, mla_w_uq, mla_w_dkv, mla_g_kv, mla_w_uk, mla_w_uv, mla_w_o,
           pool_w, pool_scale,
           s5_lam_re, s5_lam_im, s5_log_dt, s5_b_re, s5_b_im, s5_c_re, s5_c_im, s5_d, s5_w_glu):
    bp, tp, d = x_prompt.shape
    bs, ts, _ = x_sample.shape
    hp = x_prompt.reshape(bp * tp, d)
    hs = x_sample.reshape(bs * ts, d)
    cos_p, sin_p = _rope_tables(jnp.arange(tp))
    cos_s, sin_s = _rope_tables(PAST_LEN + jnp.arange(ts))
    cos_s, sin_s = jnp.tile(cos_s, (bs, 1)), jnp.tile(sin_s, (bs, 1))
    cache_krope_t = cache_krope.transpose(0, 1, 3, 2)
    lat_p, kr_p, lat_s, kr_s = [], [], [], []
    pool_p, pool_s = [], []
    s5r_p, s5i_p, s5r_s, s5i_s = [], [], [], []
    tm_p, tm_s = 1024, bs * ts
    rows_tile = 512
    rows_tile_s = min(rows_tile, bs * ts)

    def ffn(h, g_pre, g_post, i, k, tm):
        tf = 256 if tm > 512 else 512
        return _half_ffn(h, g_pre, g_post, ffn_w_gate, ffn_w_up, ffn_w_down, layer=i, half=k, tm=tm, tf=tf)

    for i in range(DEPTH):
        g = norm_gains[i]
        j = i // N_MIXERS
        kind = i % N_MIXERS
        hp = ffn(hp, g[0], g[1], i, 0, tm_p)
        hs = ffn(hs, g[0], g[1], i, 0, tm_s)
        if kind == 0:
            wcat, wuq, wkv_h, wuk_t, wuv_h, wo = _mla_weights(
                mla_w_dq[j], mla_w_uq[j], mla_w_dkv[j], mla_w_uk[j], mla_w_uv[j], mla_w_o[j])
            proj = functools.partial(_mla_proj, g2=g[2], g_q=mla_g_q[j], g_kv=mla_g_kv[j], wcat=wcat, wuq=wuq)
            qn, qr, lat, latb, kr, kr2 = proj(hp, cos_t=cos_p, sin_t=sin_p, tm=rows_tile)
            att = _prompt_attn(qn, qr, latb, kr2, wkv_h, b=bp, t=tp, tq=min(512, tp))
            hp = _out_proj(att, wo, hp, g[3], tm=rows_tile)
            lat_p.append(lat.reshape(bp, tp, KV_LORA))
            kr_p.append(kr.reshape(bp, tp, QK_ROPE))

            qn, qr, lat, _, kr, _ = proj(hs, cos_t=cos_s, sin_t=sin_s, tm=rows_tile_s)
            ql = _absorb_q(qn, wuk_t)
            ql = ql.reshape(N_HEADS, bs, ts, KV_LORA).transpose(1, 0, 2, 3).reshape(bs, N_HEADS * ts, KV_LORA)
            qrr = qr.reshape(bs, ts, N_HEADS, QK_ROPE).transpose(0, 2, 1, 3).reshape(bs, N_HEADS * ts, QK_ROPE)
            lat = lat.reshape(bs, ts, KV_LORA)
            kr = kr.reshape(bs, ts, QK_ROPE)
            o_lat = _sample_attn(ql, qrr, lat, kr, cache_latent, cache_krope_t, page_table, j)
            o_lat = o_lat.reshape(bs, N_HEADS, ts, KV_LORA).transpose(1, 0, 2, 3).reshape(N_HEADS, bs * ts, KV_LORA)
            hs = _out_proj(_expand_o(o_lat, wuv_h), wo, hs, g[3], tm=rows_tile_s)
            lat_s.append(lat)
            kr_s.append(kr)
        elif kind == 1:
            wp = pool_w[j].astype(BF16)
            hp, st = _pool_prompt(hp, g[2], g[3], pool_scale[j], wp, b=bp, t=tp, tm=rows_tile)
            pool_p.append(st)
            h_tm = hs.reshape(bs, ts, d).transpose(1, 0, 2)
            h_tm, st_tm = _pool_sample(state_pool[j].transpose(1, 0, 2), h_tm, g[2], g[3], pool_scale[j], wp,
                                       nb=min(32, bs), start_pos=PAST_LEN)
            hs = h_tm.transpose(1, 0, 2).reshape(bs * ts, d)
            pool_s.append(st_tm.transpose(1, 0, 2))
        else:
            a_re, a_im, bb_re, bb_im = _s5_params(s5_lam_re[j], s5_lam_im[j], s5_log_dt[j],
                                                  s5_b_re[j].transpose(2, 0, 1), s5_b_im[j].transpose(2, 0, 1))
            a_blk, b_blk, c_blk = _s5_block_mats(a_re, a_im, bb_re, bb_im, s5_c_re[j], s5_c_im[j])
            wglu = s5_w_glu[j].astype(BF16)
            n_state = S5_GROUPS * S5_STATE

            rp = PROMPT_ROWS_PER_STEP
            h_tm = hp.reshape(bp, tp, d).transpose(1, 0, 2)
            h_pad = jnp.pad(h_tm, ((0, 0), (0, rp - bp), (0, 0))).reshape(tp * rp, d)
            zero_state = jnp.zeros((rp, n_state), F32)
            y, sr, si = _s5_scan(_rms_call(h_pad, g[2], tm=1024), a_blk, b_blk, c_blk, s5_d[j], zero_state, zero_state,
                                 rows_per_step=rp, steps=PROMPT_SCAN_STEPS)
            y = y.reshape(tp, rp, d)[:, :bp].reshape(tp * bp, d)
            h_tm = _glu_out(y, wglu, h_tm.reshape(tp * bp, d), g[3], tm=rows_tile)
            hp = h_tm.reshape(tp, bp, d).transpose(1, 0, 2).reshape(bp * tp, d)
            s5r_p.append(sr[:bp].reshape(bp, S5_GROUPS, S5_STATE))
            s5i_p.append(si[:bp].reshape(bp, S5_GROUPS, S5_STATE))

            h_tm = hs.reshape(bs, ts, d).transpose(1, 0, 2).reshape(ts * bs, d)
            y, sr, si = _s5_scan(_rms_call(h_tm, g[2], tm=rows_tile_s), a_blk, b_blk, c_blk, s5_d[j],
                                 state_s5_re[j].reshape(bs, n_state), state_s5_im[j].reshape(bs, n_state),
                                 rows_per_step=bs, steps=ts)
            h_tm = _glu_out(y, wglu, h_tm, g[3], tm=rows_tile_s)
            hs = h_tm.reshape(ts, bs, d).transpose(1, 0, 2).reshape(bs * ts, d)
            s5r_s.append(sr.reshape(bs, S5_GROUPS, S5_STATE))
            s5i_s.append(si.reshape(bs, S5_GROUPS, S5_STATE))
        hp = ffn(hp, g[4], g[5], i, 1, tm_p)
        hs = ffn(hs, g[4], g[5], i, 1, tm_s)
    return (hp.reshape(bp, tp, d), hs.reshape(bs, ts, d),
            jnp.stack(lat_p), jnp.stack(kr_p), jnp.stack(lat_s), jnp.stack(kr_s),
            jnp.stack(pool_p), jnp.stack(pool_s),
            jnp.stack(s5r_p), jnp.stack(s5i_p), jnp.stack(s5r_s), jnp.stack(s5i_s))
```

```python
import functools
import math

import jax
import jax.numpy as jnp
from jax import lax
from jax.experimental import pallas as pl
from jax.experimental.pallas import tpu as pltpu

D_MODEL = 2048
DEPTH = 4
PAST_LEN = 8192
PAGE_SIZE = 128
N_MIXERS = 3
D_FF = 5632
MACARON_WEIGHT = 0.5
NORM_EPS = 1e-6
N_HEADS = 16
Q_LORA = 512
KV_LORA = 512
QK_NOPE = 128
QK_ROPE = 64
V_DIM = 128
ROPE_THETA = 10000.0
ATTN_SCALE = 1.0 / math.sqrt(QK_NOPE + QK_ROPE)
NEG_INF = -1e30
POOL_WINDOWS = (2, 4, 8, 16)
POOL_GROUP = D_MODEL // len(POOL_WINDOWS)
POOL_HIST = max(POOL_WINDOWS) - 1
S5_GROUP = 16
S5_GROUPS = D_MODEL // S5_GROUP
S5_STATE = 64

F32 = jnp.float32
BF16 = jnp.bfloat16

V7X_VMEM_BYTES = 64 * 1024 * 1024
VMEM_LIMIT_BYTES = V7X_VMEM_BYTES - 8 * 1024 * 1024

DOWN_COLS = 512
NORM_ROWS = 128


def _rms_rows(x, gain):
    ms = jnp.mean(x * x, axis=-1, keepdims=True)
    return x * lax.rsqrt(ms + NORM_EPS) * gain


def _cparams(*sem):
    return pltpu.CompilerParams(dimension_semantics=sem, vmem_limit_bytes=VMEM_LIMIT_BYTES)


def _const_spec(shape):
    nd = len(shape)
    return pl.BlockSpec(shape, lambda *_: (0,) * nd, pipeline_mode=pl.Buffered(1))


def _dot(a, b):
    return jnp.dot(a, b, preferred_element_type=F32)


def _dot_nt(a, b):
    return lax.dot_general(a, b, (((1,), (1,)), ((), ())), preferred_element_type=F32)


def _ffn_kernel(h_ref, gpre_ref, gpost_ref, wg_ref, wu_ref, wd_ref, o_ref, u_ref, *, n_f, tm):
    f = pl.program_id(1)
    rows = min(NORM_ROWS, tm)
    n_chunks = tm // rows

    @pl.when(f == 0)
    def _():
        def body(c, carry):
            r = pl.multiple_of(c * rows, rows)
            x = h_ref[pl.ds(r, rows), :]
            u_ref[pl.ds(r, rows), :] = _rms_rows(x, gpre_ref[...]).astype(BF16)
            o_ref[pl.ds(r, rows), :] = jnp.zeros((rows, o_ref.shape[1]), F32)
            return carry
        lax.fori_loop(0, n_chunks, body, 0)

    u = u_ref[...]
    gate = _dot(u, wg_ref[...].astype(BF16))
    up = _dot(u, wu_ref[...].astype(BF16))
    act = (gate * jax.nn.sigmoid(gate) * up).astype(BF16)
    wd = wd_ref[...].astype(BF16)
    d = o_ref.shape[1]
    for c in range(d // DOWN_COLS):
        cols = slice(c * DOWN_COLS, (c + 1) * DOWN_COLS)
        o_ref[:, cols] += _dot(act, wd[:, cols])

    @pl.when(f == n_f - 1)
    def _():
        def body(c, carry):
            r = pl.multiple_of(c * rows, rows)
            y = o_ref[pl.ds(r, rows), :]
            o_ref[pl.ds(r, rows), :] = (
                h_ref[pl.ds(r, rows), :] + MACARON_WEIGHT * _rms_rows(y, gpost_ref[...]))
            return carry
        lax.fori_loop(0, n_chunks, body, 0)


def _half_ffn(h, g_pre, g_post, w_gate, w_up, w_down, *, layer, half, tm, tf):
    m, d = h.shape
    n_f = D_FF // tf
    return pl.pallas_call(
        functools.partial(_ffn_kernel, n_f=n_f, tm=tm),
        grid=(m // tm, n_f),
        in_specs=[
            pl.BlockSpec((tm, d), lambda i, f: (i, 0)),
            pl.BlockSpec((1, d), lambda i, f: (0, 0)),
            pl.BlockSpec((1, d), lambda i, f: (0, 0)),
            pl.BlockSpec((None, None, d, tf), lambda i, f: (layer, half, 0, f)),
            pl.BlockSpec((None, None, d, tf), lambda i, f: (layer, half, 0, f)),
            pl.BlockSpec((None, None, tf, d), lambda i, f: (layer, half, f, 0)),
        ],
        out_specs=pl.BlockSpec((tm, d), lambda i, f: (i, 0)),
        out_shape=jax.ShapeDtypeStruct((m, d), F32),
        scratch_shapes=[pltpu.VMEM((tm, d), BF16)],
        compiler_params=_cparams("arbitrary", "arbitrary"),
        name="half_ffn",
    )(h, g_pre.reshape(1, d), g_post.reshape(1, d), w_gate, w_up, w_down)


ROPE_LANES = 128


def _mla_proj_kernel(h_ref, g2_ref, gq_ref, gkv_ref, wcat_ref, wuq_ref, cos_ref, sin_ref,
                     qn_ref, qr_ref, lat_ref, latb_ref, kr_ref, kr2_ref):
    u = _rms_rows(h_ref[...], g2_ref[...]).astype(BF16)
    r = _dot(u, wcat_ref[...])
    c = _rms_rows(r[:, :Q_LORA], gq_ref[...]).astype(BF16)
    lat = _rms_rows(r[:, Q_LORA:Q_LORA + KV_LORA], gkv_ref[...])
    lat_ref[...] = lat
    latb_ref[...] = lat.astype(BF16)
    cos = cos_ref[...]
    sin = sin_ref[...]
    k0 = Q_LORA + KV_LORA
    kr2 = r[:, k0:k0 + ROPE_LANES] * cos + r[:, k0 + ROPE_LANES:k0 + 2 * ROPE_LANES] * sin
    kr_ref[...] = kr2[:, :QK_ROPE]
    kr2_ref[...] = kr2.astype(BF16)
    q = _dot(c, wuq_ref[...])
    n_nope = N_HEADS * QK_NOPE
    n_rope = N_HEADS * QK_ROPE
    qn_ref[...] = q[:, :n_nope].astype(BF16)
    reps = n_rope // ROPE_LANES
    cos_q = jnp.concatenate([cos] * reps, axis=1)
    sin_q = jnp.concatenate([sin] * reps, axis=1)
    qr = q[:, n_nope:n_nope + n_rope] * cos_q + q[:, n_nope + n_rope:] * sin_q
    qr_ref[...] = qr.astype(BF16)


def _mla_proj(h, g2, g_q, g_kv, wcat, wuq, cos_t, sin_t, *, tm):
    m, d = h.shape
    tab_blocks = cos_t.shape[0] // tm
    n_nope = N_HEADS * QK_NOPE
    n_rope = N_HEADS * QK_ROPE
    row = lambda n: pl.BlockSpec((tm, n), lambda i: (i, 0))
    tab = pl.BlockSpec((tm, ROPE_LANES), lambda i: (i % tab_blocks, 0))
    return pl.pallas_call(
        _mla_proj_kernel,
        grid=(m // tm,),
        in_specs=[row(d), _const_spec((1, d)), _const_spec((1, Q_LORA)), _const_spec((1, KV_LORA)),
                  _const_spec(wcat.shape), _const_spec(wuq.shape), tab, tab],
        out_specs=[row(n_nope), row(n_rope), row(KV_LORA), row(KV_LORA), row(QK_ROPE), row(ROPE_LANES)],
        out_shape=[jax.ShapeDtypeStruct((m, n_nope), BF16), jax.ShapeDtypeStruct((m, n_rope), BF16),
                   jax.ShapeDtypeStruct((m, KV_LORA), F32), jax.ShapeDtypeStruct((m, KV_LORA), BF16),
                   jax.ShapeDtypeStruct((m, QK_ROPE), F32), jax.ShapeDtypeStruct((m, ROPE_LANES), BF16)],
        compiler_params=_cparams("arbitrary"),
        name="mla_proj",
    )(h, g2.reshape(1, d), g_q.reshape(1, Q_LORA), g_kv.reshape(1, KV_LORA), wcat, wuq, cos_t, sin_t)


def _prompt_attn_kernel(qn_ref, qr_ref, latb_ref, kr2_ref, wkv_ref, o_ref, k_sc, v_sc, *, tq):
    head = pl.program_id(1)
    t = latb_ref.shape[0]
    kv = _dot(latb_ref[...], wkv_ref[0])
    k_sc[:, :QK_NOPE] = kv[:, :QK_NOPE].astype(BF16)
    lane = lax.broadcasted_iota(jnp.int32, (t, ROPE_LANES), 1)
    mine = (lane // QK_ROPE) == (head % 2)
    k_sc[:, QK_NOPE:] = jnp.where(mine, kr2_ref[...], jnp.zeros_like(kr2_ref[...]))
    v_sc[...] = kv[:, QK_NOPE:].astype(BF16)

    rows = lax.broadcasted_iota(jnp.int32, (tq, tq), 0)
    cols = lax.broadcasted_iota(jnp.int32, (tq, tq), 1)
    causal = cols <= rows
    for qi in range(t // tq):
        own = slice(qi * tq, (qi + 1) * tq)
        past = slice(0, qi * tq)
        q = jnp.concatenate([qn_ref[own, :], qr_ref[own, :]], axis=1)
        s_own = jnp.where(causal, _dot_nt(q, k_sc[own, :]) * ATTN_SCALE, NEG_INF)
        m = jnp.max(s_own, axis=1, keepdims=True)
        if qi:
            s_past = _dot_nt(q, k_sc[past, :]) * ATTN_SCALE
            m = jnp.maximum(m, jnp.max(s_past, axis=1, keepdims=True))
        p_own = jnp.exp(s_own - m)
        l = jnp.sum(p_own, axis=1, keepdims=True)
        acc = _dot(p_own.astype(BF16), v_sc[own, :])
        if qi:
            p_past = jnp.exp(s_past - m)
            l = l + jnp.sum(p_past, axis=1, keepdims=True)
            acc = acc + _dot(p_past.astype(BF16), v_sc[past, :])
        o_ref[own, :] = (acc / l).astype(BF16)


def _prompt_attn(qn, qr, latb, kr2, wkv_h, *, b, t, tq):
    return pl.pallas_call(
        functools.partial(_prompt_attn_kernel, tq=tq),
        grid=(b, N_HEADS),
        in_specs=[
            pl.BlockSpec((t, QK_NOPE), lambda bi, h: (bi, h)),
            pl.BlockSpec((t, ROPE_LANES), lambda bi, h: (bi, h // 2)),
            pl.BlockSpec((t, KV_LORA), lambda bi, h: (bi, 0)),
            pl.BlockSpec((t, ROPE_LANES), lambda bi, h: (bi, 0)),
            pl.BlockSpec((1, KV_LORA, QK_NOPE + V_DIM), lambda bi, h: (h, 0, 0)),
        ],
        out_specs=pl.BlockSpec((t, V_DIM), lambda bi, h: (bi, h)),
        out_shape=jax.ShapeDtypeStruct((b * t, N_HEADS * V_DIM), BF16),
        scratch_shapes=[pltpu.VMEM((t, QK_NOPE + ROPE_LANES), BF16), pltpu.VMEM((t, V_DIM), BF16)],
        compiler_params=_cparams("arbitrary", "arbitrary"),
        name="prompt_attn",
    )(qn, qr, latb, kr2, wkv_h)


def _bmm_kernel(x_ref, w_ref, o_ref):
    o_ref[...] = _dot(x_ref[...], w_ref[...]).astype(o_ref.dtype)


def _absorb_q(qn, wuk_t):
    m = qn.shape[0]
    return pl.pallas_call(
        _bmm_kernel,
        grid=(N_HEADS,),
        in_specs=[pl.BlockSpec((m, QK_NOPE), lambda h: (0, h)),
                  pl.BlockSpec((None, QK_NOPE, KV_LORA), lambda h: (h, 0, 0))],
        out_specs=pl.BlockSpec((None, m, KV_LORA), lambda h: (h, 0, 0)),
        out_shape=jax.ShapeDtypeStruct((N_HEADS, m, KV_LORA), BF16),
        compiler_params=_cparams("arbitrary"),
        name="absorb_q",
    )(qn, wuk_t)


def _expand_o(o_lat, wuv_h):
    m = o_lat.shape[1]
    return pl.pallas_call(
        _bmm_kernel,
        grid=(N_HEADS,),
        in_specs=[pl.BlockSpec((None, m, KV_LORA), lambda h: (h, 0, 0)),
                  pl.BlockSpec((None, KV_LORA, V_DIM), lambda h: (h, 0, 0))],
        out_specs=pl.BlockSpec((m, V_DIM), lambda h: (0, h)),
        out_shape=jax.ShapeDtypeStruct((m, N_HEADS * V_DIM), BF16),
        compiler_params=_cparams("arbitrary"),
        name="expand_o",
    )(o_lat, wuv_h)


PAGES_PER_STEP = 64


def _sample_attn_kernel(pt_ref, ql_ref, qr_ref, latn_ref, krn_ref, *rest, n_steps, t_new):
    del pt_ref
    lat_refs = rest[:PAGES_PER_STEP]
    kr_refs = rest[PAGES_PER_STEP:2 * PAGES_PER_STEP]
    o_ref, m_sc, l_sc, acc_sc = rest[2 * PAGES_PER_STEP:]
    j = pl.program_id(1)
    rows = ql_ref.shape[1]

    @pl.when(j == 0)
    def _():
        m_sc[...] = jnp.full(m_sc.shape, NEG_INF, F32)
        l_sc[...] = jnp.zeros(l_sc.shape, F32)
        acc_sc[...] = jnp.zeros(acc_sc.shape, F32)

    ql = ql_ref[0]
    qr = qr_ref[0]
    lats = [lat_refs[i][0, 0].astype(BF16) for i in range(PAGES_PER_STEP)]
    kr_t = jnp.concatenate([kr_refs[i][0, 0] for i in range(PAGES_PER_STEP)], axis=1).astype(BF16)
    s = (jnp.concatenate([_dot_nt(ql, lats[i]) for i in range(PAGES_PER_STEP)], axis=1)
         + _dot(qr, kr_t)) * ATTN_SCALE
    m_prev = m_sc[...]
    m_new = jnp.maximum(m_prev, jnp.max(s, axis=1, keepdims=True))
    alpha = jnp.exp(m_prev - m_new)
    p = jnp.exp(s - m_new)
    l_new = alpha * l_sc[...] + jnp.sum(p, axis=1, keepdims=True)
    pb = p.astype(BF16)
    pv = _dot(pb[:, :PAGE_SIZE], lats[0])
    for i in range(1, PAGES_PER_STEP):
        pv += _dot(pb[:, i * PAGE_SIZE:(i + 1) * PAGE_SIZE], lats[i])
    acc_new = alpha * acc_sc[...] + pv
    m_sc[...] = m_new
    l_sc[...] = l_new
    acc_sc[...] = acc_new

    @pl.when(j == n_steps - 1)
    def _():
        latn = latn_ref[0].astype(BF16).astype(F32)
        krn = krn_ref[0].astype(BF16).astype(F32)
        qlf = ql.astype(F32)
        qrf = qr.astype(F32)
        tok = lax.broadcasted_iota(jnp.int32, (rows, 1), 0) % t_new
        s_new = []
        for k in range(t_new):
            sk = (jnp.sum(qlf * latn[k:k + 1, :], axis=1, keepdims=True)
                  + jnp.sum(qrf * krn[k:k + 1, :], axis=1, keepdims=True)) * ATTN_SCALE
            s_new.append(jnp.where(k <= tok, sk, NEG_INF))
        m_fin = m_new
        for sk in s_new:
            m_fin = jnp.maximum(m_fin, sk)
        a_fin = jnp.exp(m_new - m_fin)
        l_fin = a_fin * l_new
        acc = a_fin * acc_new
        for k in range(t_new):
            pk = jnp.exp(s_new[k] - m_fin)
            l_fin = l_fin + pk
            acc = acc + pk.astype(BF16).astype(F32) * latn[k:k + 1, :]
        o_ref[0] = (acc / l_fin).astype(BF16)


def _sample_attn(ql, qr, lat_new, kr_new, cache_latent, cache_krope_t, page_table, layer):
    b, rows, _ = ql.shape
    t_new = lat_new.shape[1]
    n_steps = page_table.shape[1] // PAGES_PER_STEP

    def page(i):
        return lambda bi, j, pt: (layer, pt[bi, j * PAGES_PER_STEP + i], 0, 0)

    per_seq = lambda n, w: pl.BlockSpec((1, n, w), lambda bi, j, pt: (bi, 0, 0))
    grid_spec = pltpu.PrefetchScalarGridSpec(
        num_scalar_prefetch=1,
        grid=(b, n_steps),
        in_specs=[per_seq(rows, KV_LORA), per_seq(rows, QK_ROPE), per_seq(t_new, KV_LORA), per_seq(t_new, QK_ROPE)]
        + [pl.BlockSpec((1, 1, PAGE_SIZE, KV_LORA), page(i)) for i in range(PAGES_PER_STEP)]
        + [pl.BlockSpec((1, 1, QK_ROPE, PAGE_SIZE), page(i)) for i in range(PAGES_PER_STEP)],
        out_specs=per_seq(rows, KV_LORA),
        scratch_shapes=[pltpu.VMEM((rows, 1), F32), pltpu.VMEM((rows, 1), F32), pltpu.VMEM((rows, KV_LORA), F32)],
    )
    return pl.pallas_call(
        functools.partial(_sample_attn_kernel, n_steps=n_steps, t_new=t_new),
        grid_spec=grid_spec,
        out_shape=jax.ShapeDtypeStruct((b, rows, KV_LORA), BF16),
        compiler_params=_cparams("arbitrary", "arbitrary"),
        name="sample_attn",
    )(page_table, ql, qr, lat_new, kr_new, *([cache_latent] * PAGES_PER_STEP), *([cache_krope_t] * PAGES_PER_STEP))


def _out_proj_kernel(x_ref, w_ref, h_ref, g_ref, o_ref):
    y = _dot(x_ref[...], w_ref[...])
    o_ref[...] = h_ref[...] + _rms_rows(y, g_ref[...])


def _out_proj(x, w, h, g, *, tm):
    m, d = h.shape
    row = lambda n: pl.BlockSpec((tm, n), lambda i: (i, 0))
    return pl.pallas_call(
        _out_proj_kernel,
        grid=(m // tm,),
        in_specs=[row(x.shape[1]), _const_spec(w.shape), row(d), _const_spec((1, d))],
        out_specs=row(d),
        out_shape=jax.ShapeDtypeStruct((m, d), F32),
        compiler_params=_cparams("arbitrary"),
        name="out_proj",
    )(x, w, h, g.reshape(1, d))


HIST_ROWS = 16


def _pool_mix_cols(window_sum, u_cols, cnt, w_g, scale_cols):
    pooled = window_sum / cnt - u_cols
    return _dot(pooled.astype(BF16), w_g) * scale_cols


def _pool_prompt_kernel(h_ref, g2_ref, g3_ref, scale_ref, w_ref, o_ref, st_ref, ubuf, *, tm, n_t):
    ti = pl.program_id(1)

    @pl.when(ti == 0)
    def _():
        ubuf[0:HIST_ROWS, :] = jnp.zeros((HIST_ROWS, ubuf.shape[1]), F32)

    @pl.when(ti > 0)
    def _():
        ubuf[0:HIST_ROWS, :] = ubuf[tm:tm + HIST_ROWS, :]

    h = h_ref[...]
    ubuf[HIST_ROWS:HIST_ROWS + tm, :] = _rms_rows(h, g2_ref[...])
    pos = ti * tm + lax.broadcasted_iota(jnp.int32, (tm, 1), 0)
    for g, w in enumerate(POOL_WINDOWS):
        cols = slice(g * POOL_GROUP, (g + 1) * POOL_GROUP)
        u_cols = ubuf[HIST_ROWS:HIST_ROWS + tm, cols]
        win = u_cols
        for k in range(1, w):
            win = win + ubuf[HIST_ROWS - k:HIST_ROWS - k + tm, cols]
        cnt = jnp.minimum(pos + 1, w).astype(F32)
        o_ref[:, cols] = _pool_mix_cols(win, u_cols, cnt, w_ref[g], scale_ref[:, cols])
    o_ref[...] = h + _rms_rows(o_ref[...], g3_ref[...])

    @pl.when(ti == n_t - 1)
    def _():
        st_ref[0] = ubuf[HIST_ROWS + tm - POOL_HIST:HIST_ROWS + tm, :]


def _pool_prompt(h, g2, g3, scale, w_bf, *, b, t, tm):
    d = h.shape[1]
    n_t = t // tm
    return pl.pallas_call(
        functools.partial(_pool_prompt_kernel, tm=tm, n_t=n_t),
        grid=(b, n_t),
        in_specs=[pl.BlockSpec((tm, d), lambda bi, ti: (bi * n_t + ti, 0)),
                  _const_spec((1, d)), _const_spec((1, d)), _const_spec((1, d)), _const_spec(w_bf.shape)],
        out_specs=[pl.BlockSpec((tm, d), lambda bi, ti: (bi * n_t + ti, 0)),
                   pl.BlockSpec((1, POOL_HIST, d), lambda bi, ti: (bi, 0, 0))],
        out_shape=[jax.ShapeDtypeStruct(h.shape, F32), jax.ShapeDtypeStruct((b, POOL_HIST, d), F32)],
        scratch_shapes=[pltpu.VMEM((HIST_ROWS + tm, d), F32)],
        compiler_params=_cparams("arbitrary", "arbitrary"),
        name="pool_prompt",
    )(h, g2.reshape(1, d), g3.reshape(1, d), scale.reshape(1, d), w_bf)


def _pool_sample_kernel(hist_ref, h_ref, g2_ref, g3_ref, scale_ref, w_ref, o_ref, st_ref, mix_sc, *, start_pos):
    t_new = h_ref.shape[0]
    nb = h_ref.shape[1]
    hs = [h_ref[t] for t in range(t_new)]
    ext = [hist_ref[k] for k in range(POOL_HIST)] + [_rms_rows(x, g2_ref[...]) for x in hs]
    for t in range(t_new):
        for g, w in enumerate(POOL_WINDOWS):
            cols = slice(g * POOL_GROUP, (g + 1) * POOL_GROUP)
            u_cols = ext[POOL_HIST + t][:, cols]
            win = u_cols
            for k in range(1, w):
                win = win + ext[POOL_HIST + t - k][:, cols]
            cnt = float(min(start_pos + t + 1, w))
            mix_sc[t * nb:(t + 1) * nb, cols] = _pool_mix_cols(win, u_cols, cnt, w_ref[g], scale_ref[:, cols])
    for t in range(t_new):
        o_ref[t] = hs[t] + _rms_rows(mix_sc[t * nb:(t + 1) * nb, :], g3_ref[...])
    for k in range(POOL_HIST):
        st_ref[k] = ext[t_new + k]


def _pool_sample(hist_tm, h_tm, g2, g3, scale, w_bf, *, nb, start_pos):
    t_new, b, d = h_tm.shape
    seq = lambda n: pl.BlockSpec((n, nb, d), lambda i: (0, i, 0))
    return pl.pallas_call(
        functools.partial(_pool_sample_kernel, start_pos=start_pos),
        grid=(b // nb,),
        in_specs=[seq(POOL_HIST), seq(t_new), _const_spec((1, d)), _const_spec((1, d)), _const_spec((1, d)),
                  _const_spec(w_bf.shape)],
        out_specs=[seq(t_new), seq(POOL_HIST)],
        out_shape=[jax.ShapeDtypeStruct(h_tm.shape, F32), jax.ShapeDtypeStruct(hist_tm.shape, F32)],
        scratch_shapes=[pltpu.VMEM((t_new * nb, d), F32)],
        compiler_params=_cparams("arbitrary"),
        name="pool_sample",
    )(hist_tm, h_tm, g2.reshape(1, d), g3.reshape(1, d), scale.reshape(1, d), w_bf)


def _s5_params_kernel(lr_ref, li_ref, logdt_ref, br_ref, bi_ref, ar_ref, ai_ref, bbr_ref, bbi_ref):
    lr = lr_ref[...]
    li = li_ref[...]
    dt = jnp.exp(logdt_ref[...])
    mag = jnp.exp(lr * dt)
    a_re = mag * jnp.cos(li * dt)
    a_im = mag * jnp.sin(li * dt)
    den = lr * lr + li * li
    nr = a_re - 1.0
    ni = a_im
    zoh_re = (nr * lr + ni * li) / den
    zoh_im = (ni * lr - nr * li) / den
    ar_ref[...] = a_re
    ai_ref[...] = a_im
    for c in range(S5_GROUP):
        b_re = br_ref[c]
        b_im = bi_ref[c]
        bbr_ref[c] = zoh_re * b_re - zoh_im * b_im
        bbi_ref[c] = zoh_re * b_im + zoh_im * b_re


def _s5_params(lam_re, lam_im, log_dt, b_re_cgp, b_im_cgp):
    g, p = lam_re.shape
    full = lambda shape: pl.BlockSpec(shape, lambda: (0,) * len(shape))
    gp = jax.ShapeDtypeStruct((g, p), F32)
    cgp = jax.ShapeDtypeStruct(b_re_cgp.shape, F32)
    return pl.pallas_call(
        _s5_params_kernel,
        in_specs=[full((g, p)), full((g, p)), full((g, 1)), full(b_re_cgp.shape), full(b_im_cgp.shape)],
        out_specs=[full((g, p)), full((g, p)), full(b_re_cgp.shape), full(b_re_cgp.shape)],
        out_shape=[gp, gp, cgp, cgp],
        name="s5_params",
    )(lam_re, lam_im, log_dt.reshape(g, 1), b_re_cgp, b_im_cgp)


def _rms_kernel(h_ref, g_ref, o_ref):
    o_ref[...] = _rms_rows(h_ref[...], g_ref[...])


def _rms_call(h, g, *, tm):
    m, d = h.shape
    row = pl.BlockSpec((tm, d), lambda i: (i, 0))
    return pl.pallas_call(
        _rms_kernel, grid=(m // tm,), in_specs=[row, _const_spec((1, d))], out_specs=row,
        out_shape=jax.ShapeDtypeStruct((m, d), F32), compiler_params=_cparams("arbitrary"), name="rms_rows",
    )(h, g.reshape(1, d))


S5_BLOCK_GROUPS = 16
S5_BLOCK_IN = S5_BLOCK_GROUPS * S5_GROUP
S5_BLOCK_STATE = S5_BLOCK_GROUPS * S5_STATE
S5_BLOCKS = S5_GROUPS // S5_BLOCK_GROUPS
BLOCKS_PER_STEP = 2


def _s5_scan_kernel(u_ref, a_ref, b_ref, c_ref, d_ref, h0r_ref, h0i_ref, y_ref, hr_ref, hi_ref,
                    buf, cr_sc, ci_sc, *, rows_per_step, steps, n_chunks):
    c = pl.program_id(1)
    rps = rows_per_step
    ns = S5_BLOCK_STATE

    @pl.when(c == 0)
    def _():
        cr_sc[...] = h0r_ref[...]
        ci_sc[...] = h0i_ref[...]

    for sb in range(BLOCKS_PER_STEP):
        ucols = slice(sb * S5_BLOCK_IN, (sb + 1) * S5_BLOCK_IN)
        scols = slice(sb * ns, (sb + 1) * ns)
        u = u_ref[:, ucols]
        buf[sb] = _dot(u.astype(BF16), b_ref[sb])
        a = a_ref[sb]
        a_re = jnp.broadcast_to(a[:, :ns], (rps, ns))
        a_im = jnp.broadcast_to(a[:, ns:], (rps, ns))
        h_re = cr_sc[:, scols]
        h_im = ci_sc[:, scols]
        for t in range(steps):
            rws = slice(t * rps, (t + 1) * rps)
            n_re = a_re * h_re - a_im * h_im + buf[sb, rws, :ns]
            n_im = a_re * h_im + a_im * h_re + buf[sb, rws, ns:]
            buf[sb, rws, :ns] = n_re
            buf[sb, rws, ns:] = n_im
            h_re, h_im = n_re, n_im
        cr_sc[:, scols] = h_re
        ci_sc[:, scols] = h_im
        y = _dot(buf[sb].astype(BF16), c_ref[sb]) + d_ref[:, ucols] * u
        y_ref[:, ucols] = jax.nn.gelu(y).astype(BF16)

    @pl.when(c == n_chunks - 1)
    def _():
        hr_ref[...] = cr_sc[...]
        hi_ref[...] = ci_sc[...]


def _s5_scan(u_tm, a_blk, b_blk, c_blk, d_skip, h0_re, h0_im, *, rows_per_step, steps):
    m, d = u_tm.shape
    rows = rows_per_step * steps
    n_chunks = m // rows
    ns = S5_BLOCK_STATE
    bps = BLOCKS_PER_STEP
    st = pl.BlockSpec((rows_per_step, bps * ns), lambda k, c: (0, k))
    st_shape = jax.ShapeDtypeStruct((rows_per_step, S5_BLOCKS * ns), F32)
    return pl.pallas_call(
        functools.partial(_s5_scan_kernel, rows_per_step=rows_per_step, steps=steps, n_chunks=n_chunks),
        grid=(S5_BLOCKS // bps, n_chunks),
        in_specs=[pl.BlockSpec((rows, bps * S5_BLOCK_IN), lambda k, c: (c, k)),
                  pl.BlockSpec((bps, 1, 2 * ns), lambda k, c: (k, 0, 0)),
                  pl.BlockSpec((bps, S5_BLOCK_IN, 2 * ns), lambda k, c: (k, 0, 0)),
                  pl.BlockSpec((bps, 2 * ns, S5_BLOCK_IN), lambda k, c: (k, 0, 0)),
                  pl.BlockSpec((1, bps * S5_BLOCK_IN), lambda k, c: (0, k)),
                  st, st],
        out_specs=[pl.BlockSpec((rows, bps * S5_BLOCK_IN), lambda k, c: (c, k)), st, st],
        out_shape=[jax.ShapeDtypeStruct((m, d), BF16), st_shape, st_shape],
        scratch_shapes=[pltpu.VMEM((bps, rows, 2 * ns), F32), pltpu.VMEM((rows_per_step, bps * ns), F32),
                        pltpu.VMEM((rows_per_step, bps * ns), F32)],
        compiler_params=_cparams("arbitrary", "arbitrary"),
        name="s5_scan",
    )(u_tm, a_blk, b_blk, c_blk, d_skip.reshape(1, d), h0_re, h0_im)


def _glu_kernel(y_ref, w_ref, h_ref, g_ref, o_ref):
    d = o_ref.shape[1]
    y = y_ref[...]
    z1 = _dot(y, w_ref[:, :d])
    z2 = _dot(y, w_ref[:, d:])
    o_ref[...] = h_ref[...] + _rms_rows(z1 * jax.nn.sigmoid(z2), g_ref[...])


def _glu_out(y, w_bf, h, g, *, tm):
    m, d = h.shape
    row = lambda n: pl.BlockSpec((tm, n), lambda i: (i, 0))
    return pl.pallas_call(
        _glu_kernel,
        grid=(m // tm,),
        in_specs=[row(d), _const_spec(w_bf.shape), row(d), _const_spec((1, d))],
        out_specs=row(d),
        out_shape=jax.ShapeDtypeStruct((m, d), F32),
        compiler_params=_cparams("arbitrary"),
        name="glu_out",
    )(y, w_bf, h, g.reshape(1, d))


def _rope_tables(pos):
    half = QK_ROPE // 2
    inv = ROPE_THETA ** (-jnp.arange(half, dtype=F32) / half)
    ang = pos.astype(F32)[:, None] * inv[None, :]
    reps = ROPE_LANES // half
    return jnp.tile(jnp.cos(ang), (1, reps)), jnp.tile(jnp.sin(ang), (1, reps))


def _rot_cols(w):
    half = QK_ROPE // 2
    return jnp.concatenate([-w[..., half:], w[..., :half]], axis=-1)


def _mla_weights(w_dq, w_uq, w_dkv, w_uk, w_uv, w_o):
    r = w_uq.shape[0]
    wr = w_uq[..., QK_NOPE:]
    wuq = jnp.concatenate([w_uq[..., :QK_NOPE].reshape(r, -1), wr.reshape(r, -1), _rot_cols(wr).reshape(r, -1)],
                          axis=1).astype(BF16)
    wk = w_dkv[:, KV_LORA:]
    wkr = _rot_cols(wk)
    wcat = jnp.concatenate([w_dq, w_dkv[:, :KV_LORA], wk, wk, wkr, wkr], axis=1).astype(BF16)
    wuk_t = w_uk.transpose(1, 2, 0).astype(BF16)
    wuv_h = w_uv.transpose(1, 0, 2).astype(BF16)
    wkv_h = jnp.concatenate([w_uk.transpose(1, 0, 2).astype(BF16), wuv_h], axis=2)
    return wcat, wuq, wkv_h, wuk_t, wuv_h, w_o.astype(BF16)


def _s5_block_mats(a_re, a_im, bb_re, bb_im, c_re, c_im):
    nb, gb, ng, ns = S5_BLOCKS, S5_BLOCK_GROUPS, S5_GROUP, S5_STATE
    eye = jnp.eye(gb, dtype=F32)
    bb = jnp.stack([bb_re, bb_im])
    bb = bb.reshape(2, ng, nb, gb, ns).transpose(2, 3, 1, 0, 4)
    b_blk = (bb[:, :, :, :, None, :] * eye[None, :, None, None, :, None]).reshape(nb, gb * ng, 2 * gb * ns)
    cc = jnp.stack([c_re, -c_im])
    cc = cc.reshape(2, nb, gb, ng, ns).transpose(1, 0, 2, 4, 3)
    c_blk = (cc[:, :, :, :, None, :] * eye[None, None, :, None, :, None]).reshape(nb, 2 * gb * ns, gb * ng)
    a_blk = jnp.concatenate([a_re.reshape(nb, gb * ns), a_im.reshape(nb, gb * ns)], axis=1)
    return a_blk.reshape(nb, 1, 2 * gb * ns), b_blk.astype(BF16), c_blk.astype(BF16)


PROMPT_ROWS_PER_STEP = 8
PROMPT_SCAN_STEPS = 128


def kernel(x_prompt, x_sample, cache_latent, cache_krope, page_table, state_pool, state_s5_re, state_s5_im,
           norm_gains, ffn_w_gate, ffn_w_up, ffn_w_down,
           mla_w_dq, mla_g_q, mla_w_uq, mla_w_dkv, mla_g_kv, mla_w_uk, mla_w_uv, mla_w_o,
           pool_w, pool_scale,
           s5_lam_re, s5_lam_im, s5_log_dt, s5_b_re, s5_b_im, s5_c_re, s5_c_im, s5_d, s5_w_glu):
    bp, tp, d = x_prompt.shape
    bs, ts, _ = x_sample.shape
    hp = x_prompt.reshape(bp * tp, d)
    hs = x_sample.reshape(bs * ts, d)
    cos_p, sin_p = _rope_tables(jnp.arange(tp))
    cos_s, sin_s = _rope_tables(PAST_LEN + jnp.arange(ts))
    cos_s, sin_s = jnp.tile(cos_s, (bs, 1)), jnp.tile(sin_s, (bs, 1))
    cache_krope_t = cache_krope.transpose(0, 1, 3, 2)
    lat_p, kr_p, lat_s, kr_s = [], [], [], []
    pool_p, pool_s = [], []
    s5r_p, s5i_p, s5r_s, s5i_s = [], [], [], []
    tm_p, tm_s = 1024, bs * ts
    rows_tile = 512
    rows_tile_s = min(rows_tile, bs * ts)

    def ffn(h, g_pre, g_post, i, k, tm):
        tf = 256 if tm > 512 else 512
        return _half_ffn(h, g_pre, g_post, ffn_w_gate, ffn_w_up, ffn_w_down, layer=i, half=k, tm=tm, tf=tf)

    for i in range(DEPTH):
        g = norm_gains[i]
        j = i // N_MIXERS
        kind = i % N_MIXERS
        hp = ffn(hp, g[0], g[1], i, 0, tm_p)
        hs = ffn(hs, g[0], g[1], i, 0, tm_s)
        if kind == 0:
            wcat, wuq, wkv_h, wuk_t, wuv_h, wo = _mla_weights(
                mla_w_dq[j], mla_w_uq[j], mla_w_dkv[j], mla_w_uk[j], mla_w_uv[j], mla_w_o[j])
            proj = functools.partial(_mla_proj, g2=g[2], g_q=mla_g_q[j], g_kv=mla_g_kv[j], wcat=wcat, wuq=wuq)
            qn, qr, lat, latb, kr, kr2 = proj(hp, cos_t=cos_p, sin_t=sin_p, tm=rows_tile)
            att = _prompt_attn(qn, qr, latb, kr2, wkv_h, b=bp, t=tp, tq=min(512, tp))
            hp = _out_proj(att, wo, hp, g[3], tm=rows_tile)
            lat_p.append(lat.reshape(bp, tp, KV_LORA))
            kr_p.append(kr.reshape(bp, tp, QK_ROPE))

            qn, qr, lat, _, kr, _ = proj(hs, cos_t=cos_s, sin_t=sin_s, tm=rows_tile_s)
            ql = _absorb_q(qn, wuk_t)
            ql = ql.reshape(N_HEADS, bs, ts, KV_LORA).transpose(1, 0, 2, 3).reshape(bs, N_HEADS * ts, KV_LORA)
            qrr = qr.reshape(bs, ts, N_HEADS, QK_ROPE).transpose(0, 2, 1, 3).reshape(bs, N_HEADS * ts, QK_ROPE)
            lat = lat.reshape(bs, ts, KV_LORA)
            kr = kr.reshape(bs, ts, QK_ROPE)
            o_lat = _sample_attn(ql, qrr, lat, kr, cache_latent, cache_krope_t, page_table, j)
            o_lat = o_lat.reshape(bs, N_HEADS, ts, KV_LORA).transpose(1, 0, 2, 3).reshape(N_HEADS, bs * ts, KV_LORA)
            hs = _out_proj(_expand_o(o_lat, wuv_h), wo, hs, g[3], tm=rows_tile_s)
            lat_s.append(lat)
            kr_s.append(kr)
        elif kind == 1:
            wp = pool_w[j].astype(BF16)
            hp, st = _pool_prompt(hp, g[2], g[3], pool_scale[j], wp, b=bp, t=tp, tm=rows_tile)
            pool_p.append(st)
            h_tm = hs.reshape(bs, ts, d).transpose(1, 0, 2)
            h_tm, st_tm = _pool_sample(state_pool[j].transpose(1, 0, 2), h_tm, g[2], g[3], pool_scale[j], wp,
                                       nb=min(32, bs), start_pos=PAST_LEN)
            hs = h_tm.transpose(1, 0, 2).reshape(bs * ts, d)
            pool_s.append(st_tm.transpose(1, 0, 2))
        else:
            a_re, a_im, bb_re, bb_im = _s5_params(s5_lam_re[j], s5_lam_im[j], s5_log_dt[j],
                                                  s5_b_re[j].transpose(2, 0, 1), s5_b_im[j].transpose(2, 0, 1))
            a_blk, b_blk, c_blk = _s5_block_mats(a_re, a_im, bb_re, bb_im, s5_c_re[j], s5_c_im[j])
            wglu = s5_w_glu[j].astype(BF16)
            n_state = S5_GROUPS * S5_STATE

            rp = PROMPT_ROWS_PER_STEP
            h_tm = hp.reshape(bp, tp, d).transpose(1, 0, 2)
            h_pad = jnp.pad(h_tm, ((0, 0), (0, rp - bp), (0, 0))).reshape(tp * rp, d)
            zero_state = jnp.zeros((rp, n_state), F32)
            y, sr, si = _s5_scan(_rms_call(h_pad, g[2], tm=1024), a_blk, b_blk, c_blk, s5_d[j], zero_state, zero_state,
                                 rows_per_step=rp, steps=PROMPT_SCAN_STEPS)
            y = y.reshape(tp, rp, d)[:, :bp].reshape(tp * bp, d)
            h_tm = _glu_out(y, wglu, h_tm.reshape(tp * bp, d), g[3], tm=rows_tile)
            hp = h_tm.reshape(tp, bp, d).transpose(1, 0, 2).reshape(bp * tp, d)
            s5r_p.append(sr[:bp].reshape(bp, S5_GROUPS, S5_STATE))
            s5i_p.append(si[:bp].reshape(bp, S5_GROUPS, S5_STATE))

            h_tm = hs.reshape(bs, ts, d).transpose(1, 0, 2).reshape(ts * bs, d)
            y, sr, si = _s5_scan(_rms_call(h_tm, g[2], tm=rows_tile_s), a_blk, b_blk, c_blk, s5_d[j],
                                 state_s5_re[j].reshape(bs, n_state), state_s5_im[j].reshape(bs, n_state),
                                 rows_per_step=bs, steps=ts)
            h_tm = _glu_out(y, wglu, h_tm, g[3], tm=rows_tile_s)
            hs = h_tm.reshape(ts, bs, d).transpose(1, 0, 2).reshape(bs * ts, d)
            s5r_s.append(sr.reshape(bs, S5_GROUPS, S5_STATE))
            s5i_s.append(si.reshape(bs, S5_GROUPS, S5_STATE))
        hp = ffn(hp, g[4], g[5], i, 1, tm_p)
        hs = ffn(hs, g[4], g[5], i, 1, tm_s)
    return (hp.reshape(bp, tp, d), hs.reshape(bs, ts, d),
            jnp.stack(lat_p), jnp.stack(kr_p), jnp.stack(lat_s), jnp.stack(kr_s),
            jnp.stack(pool_p), jnp.stack(pool_s),
            jnp.stack(s5r_p), jnp.stack(s5i_p), jnp.stack(s5r_s), jnp.stack(s5i_s))
```
